```python
import math
import jax, jax.numpy as jnp
from jax import lax
import numpy as np

D_MODEL = 1024
BATCH = 4
SEQ = 8192
DEPTH = 2
DEC_BATCH = 8
DEC_SEQ = 32
PAST_LEN = 2048

CHUNK = 64
N_A = DEPTH // 2
N_B = DEPTH - N_A
EPS = 1e-6

GDN_HEADS = 8
GDN_DK = 128
GDN_DV = 128
CONV_W = 4
CONV_DIM = GDN_HEADS * (2 * GDN_DK + GDN_DV)
GDN_IN = CONV_DIM + GDN_HEADS * GDN_DV + 2 * GDN_HEADS

MLA_HEADS = 8
QK_NOPE = 128
QK_ROPE = 64
V_HEAD = 128
KV_LORA = 256
Q_LORA = 384
ROPE_THETA = 10000.0
SM_SCALE = (QK_NOPE + QK_ROPE) ** -0.5
Q_BLOCK = 128

N_KEYS = 128
N_EXPERTS = N_KEYS * N_KEYS
PEER_HEADS = 8
PEER_TOPK = 16
PK_DIM = 256
PK_HALF = PK_DIM // 2
PEER_BLOCK = 256

kernel_name = 'yoco_gdn_mla_peer_stream_step'


def rmsnorm(x, g):
    xf = x.astype(jnp.float32)
    y = xf * lax.rsqrt(jnp.mean(xf * xf, axis=-1, keepdims=True) + EPS)
    return y.astype(x.dtype) * g


def l2norm(x):
    return x * lax.rsqrt(jnp.sum(x * x, axis=-1, keepdims=True) + EPS)


def rope(x, pos):
    half = x.shape[-1] // 2
    inv = ROPE_THETA ** (-jnp.arange(half, dtype=jnp.float32) / half)
    ang = pos.astype(jnp.float32)[:, None] * inv[None, :]
    ang = ang.reshape((pos.shape[0],) + (1,) * (x.ndim - 3) + (half,))
    cos, sin = jnp.cos(ang), jnp.sin(ang)
    xf = x.astype(jnp.float32)
    x1, x2 = xf[..., :half], xf[..., half:]
    return jnp.concatenate([x1 * cos - x2 * sin, x1 * sin + x2 * cos], axis=-1).astype(x.dtype)


def gated_delta_chunked(q, k, v, g, beta, s0, chunk):
    bsz, t, nh, dk = q.shape
    dv = v.shape[-1]
    n = t // chunk

    def blk(a):
        a = a.reshape((bsz, n, chunk, nh) + a.shape[3:])
        return jnp.moveaxis(a, (1, 3), (0, 2))

    qc, kc, vc, gc, bc = blk(q), blk(k), blk(v), blk(g), blk(beta)
    d = jnp.cumsum(gc, axis=-1)
    idx = jnp.arange(chunk)
    causal = idx[:, None] >= idx[None, :]
    strict = idx[:, None] > idx[None, :]
    decay_mat = jnp.exp(jnp.where(causal, d[..., :, None] - d[..., None, :], -jnp.inf))
    kb = kc * bc[..., None]
    a_low = jnp.where(strict, jnp.einsum('nbhid,nbhjd->nbhij', kb, kc) * decay_mat, 0.0)
    tri = a_low + jnp.eye(chunk, dtype=jnp.float32)
    rhs = jnp.concatenate([vc * bc[..., None], kb * jnp.exp(d)[..., None]], axis=-1)
    sol = lax.linalg.triangular_solve(tri, rhs, left_side=True, lower=True, unit_diagonal=True)
    u, w = sol[..., :dv], sol[..., dv:]
    qk = jnp.einsum('nbhid,nbhjd->nbhij', qc, kc) * decay_mat
    q_dec = qc * jnp.exp(d)[..., None]
    k_dec = kc * jnp.exp(d[..., -1:] - d)[..., None]
    last = jnp.exp(d[..., -1])

    def step(s, inp):
        u_i, w_i, qk_i, qd_i, kd_i, l_i = inp
        v_new = u_i - jnp.einsum('bhcd,bhde->bhce', w_i, s)
        o_i = jnp.einsum('bhcd,bhde->bhce', qd_i, s) + jnp.einsum('bhij,bhje->bhie', qk_i, v_new)
        s = s * l_i[..., None, None] + jnp.einsum('bhcd,bhce->bhde', kd_i, v_new)
        return s, o_i

    s, o = lax.scan(step, s0, (u, w, qk, q_dec, k_dec, last))
    o = jnp.moveaxis(o, (0, 2), (1, 3)).reshape(bsz, t, nh, dv)
    return o, s


def gdn_mixer(h, conv_buf, s0, w_in, conv_w, a_log, dt_bias, onorm_g, w_o):
    bsz, t, _ = h.shape
    proj = h @ w_in
    o1 = CONV_DIM + GDN_HEADS * GDN_DV
    qkv = proj[..., :CONV_DIM]
    gate = proj[..., CONV_DIM:o1]
    a = proj[..., o1:o1 + GDN_HEADS]
    b = proj[..., o1 + GDN_HEADS:]
    xpad = jnp.concatenate([conv_buf.astype(qkv.dtype), qkv], axis=1)
    new_buf = xpad[:, -(CONV_W - 1):]
    conv = lax.conv_general_dilated(xpad, conv_w[:, None, :], (1,), 'VALID',
                                    dimension_numbers=('NWC', 'WIO', 'NWC'),
                                    feature_group_count=CONV_DIM)
    qkv = jax.nn.silu(conv).astype(jnp.float32)
    nk = GDN_HEADS * GDN_DK
    q = l2norm(qkv[..., :nk].reshape(bsz, t, GDN_HEADS, GDN_DK)) * (GDN_DK ** -0.5)
    k = l2norm(qkv[..., nk:2 * nk].reshape(bsz, t, GDN_HEADS, GDN_DK))
    v = qkv[..., 2 * nk:].reshape(bsz, t, GDN_HEADS, GDN_DV)
    beta = jax.nn.sigmoid(b.astype(jnp.float32))
    g = -jnp.exp(a_log.astype(jnp.float32)) * jax.nn.softplus(a.astype(jnp.float32) + dt_bias.astype(jnp.float32))
    o, s = gated_delta_chunked(q, k, v, g, beta, s0.astype(jnp.float32), min(CHUNK, t))
    gate = gate.reshape(bsz, t, GDN_HEADS, GDN_DV).astype(jnp.float32)
    o = rmsnorm(o, onorm_g.astype(jnp.float32)) * jax.nn.silu(gate)
    y = o.astype(h.dtype).reshape(bsz, t, GDN_HEADS * GDN_DV) @ w_o
    return y, new_buf, s.astype(h.dtype)


def shared_latent(x, kv_src_g, w_dkv, kv_norm_g, pos):
    ckr = rmsnorm(x, kv_src_g) @ w_dkv
    return rmsnorm(ckr[..., :KV_LORA], kv_norm_g), rope(ckr[..., KV_LORA:], pos)


def attend(q_nope, q_rope, k_nope, k_rope, v, q_pos, k_pos):
    s = (jnp.einsum('bqhd,bkhd->bhqk', q_nope, k_nope)
         + jnp.einsum('bqhd,bkd->bhqk', q_rope, k_rope)).astype(jnp.float32) * SM_SCALE
    visible = (k_pos[None, :] // CHUNK) <= (q_pos[:, None] // CHUNK)
    p = jax.nn.softmax(jnp.where(visible, s, -jnp.inf), axis=-1).astype(v.dtype)
    return jnp.einsum('bhqk,bkhd->bqhd', p, v)


def mla_mixer(h, pos, k_nope, k_rope, v, k_pos, w_dq, q_norm_g, w_uq, w_o):
    bsz, t, _ = h.shape
    cq = rmsnorm(h @ w_dq, q_norm_g)
    q = (cq @ w_uq).reshape(bsz, t, MLA_HEADS, QK_NOPE + QK_ROPE)
    q_nope = q[..., :QK_NOPE]
    q_rope = rope(q[..., QK_NOPE:], pos)
    if t % Q_BLOCK == 0:
        nq = t // Q_BLOCK

        def blockify(a):
            return jnp.moveaxis(a.reshape((bsz, nq, Q_BLOCK) + a.shape[2:]), 1, 0)

        def one(args):
            qn, qr, qp = args
            return attend(qn, qr, k_nope, k_rope, v, qp, k_pos)

        o = lax.map(one, (blockify(q_nope), blockify(q_rope), pos.reshape(nq, Q_BLOCK)))
        o = jnp.moveaxis(o, 0, 1).reshape(bsz, t, MLA_HEADS, V_HEAD)
    else:
        o = attend(q_nope, q_rope, k_nope, k_rope, v, pos, k_pos)
    return o.reshape(bsz, t, MLA_HEADS * V_HEAD) @ w_o


def peer(h, w_q, sub_keys, u_tab, v_tab):
    bsz, t, dm = h.shape
    xt = h.reshape(bsz * t, dm)
    n = xt.shape[0]
    blk = min(PEER_BLOCK, n)
    npad = -(-n // blk) * blk
    xt = jnp.pad(xt, ((0, npad - n), (0, 0)))

    def one(xb):
        q = (xb @ w_q).reshape(blk, PEER_HEADS, 2, PK_HALF)
        s = jnp.einsum('thpd,hpkd->thpk', q, sub_keys).astype(jnp.float32)
        sv, si = lax.top_k(s, PEER_TOPK)
        cand = (sv[:, :, 0, :, None] + sv[:, :, 1, None, :]).reshape(blk, PEER_HEADS, PEER_TOPK * PEER_TOPK)
        cv, ci = lax.top_k(cand, PEER_TOPK)
        i1 = jnp.take_along_axis(si[:, :, 0], ci // PEER_TOPK, axis=-1)
        i2 = jnp.take_along_axis(si[:, :, 1], ci % PEER_TOPK, axis=-1)
        e = i1 * N_KEYS + i2
        gate = jax.nn.softmax(cv, axis=-1)
        act = jax.nn.gelu(jnp.einsum('td,thkd->thk', xb, u_tab[e]).astype(jnp.float32)) * gate
        return jnp.einsum('thk,thkd->td', act.astype(xb.dtype), v_tab[e])

    out = lax.map(one, xt.reshape(npad // blk, blk, dm)).reshape(npad, dm)[:n]
    return out.reshape(bsz, t, dm)


def trunk(x, c, pos, conv_bufs, delta_states, past_c, past_kr, p):
    bsz, t, _ = x.shape
    mods = jnp.einsum('bd,ldm->lbm', jax.nn.silu(c), p['w_mod']) + p['b_mod'][:, None, :]
    k_pos = jnp.arange(past_c.shape[1] + t)
    new_states, new_bufs = [], []
    for l in range(DEPTH):
        sh1, sc1, g1, sh2, sc2, g2 = jnp.split(mods[l][:, None, :], 6, axis=-1)
        h = rmsnorm(x, p['norm1_g'][l]) * (1 + sc1) + sh1
        if l < N_A:
            y, buf, s = gdn_mixer(h, conv_bufs[l], delta_states[l], p['gdn_w_in'][l], p['gdn_conv_w'][l],
                                  p['gdn_a_log'][l], p['gdn_dt_bias'][l], p['gdn_onorm_g'][l], p['gdn_w_o'][l])
            new_bufs.append(buf)
            new_states.append(s)
        else:
            if l == N_A:
                c_kv, k_r = shared_latent(x, p['kv_src_g'], p['mla_w_dkv'], p['mla_kv_norm_g'], pos)
                c_all = jnp.concatenate([past_c.astype(c_kv.dtype), c_kv], axis=1)
                kr_all = jnp.concatenate([past_kr.astype(k_r.dtype), k_r], axis=1)
                k_nope = (c_all @ p['mla_w_uk']).reshape(bsz, -1, MLA_HEADS, QK_NOPE)
                v_all = (c_all @ p['mla_w_uv']).reshape(bsz, -1, MLA_HEADS, V_HEAD)
            j = l - N_A
            y = mla_mixer(h, pos, k_nope, kr_all, v_all, k_pos, p['mla_w_dq'][j], p['mla_q_norm_g'][j],
                          p['mla_w_uq'][j], p['mla_w_o'][j])
        x = x + g1 * y
        h = rmsnorm(x, p['norm2_g'][l]) * (1 + sc2) + sh2
        x = x + g2 * peer(h, p['peer_w_q'][l], p['peer_sub_keys'][l], p['peer_u'][l], p['peer_v'][l])
    return rmsnorm(x, p['final_g']), jnp.stack(new_states), jnp.stack(new_bufs), c_kv, k_r


def setup_inputs(seed: int = 0) -> dict:
    key = jax.random.key(seed)
    ks = jax.random.split(key, 40)
    f32 = jnp.float32
    D = D_MODEL

    def nrm(i, shape, scale):
        return scale * jax.random.normal(ks[i], shape, f32)

    def gain(i, shape):
        return 1.0 + nrm(i, shape, 0.02)

    dt = jnp.exp(jax.random.uniform(ks[16], (N_A, GDN_HEADS), f32, math.log(1e-3), math.log(1e-1)))
    return {
        'x_prompt': nrm(0, (BATCH, SEQ, D), 1.0),
        'x_sample': nrm(1, (DEC_BATCH, DEC_SEQ, D), 1.0),
        'c_prompt': nrm(2, (BATCH, D), 1.0),
        'c_sample': nrm(3, (DEC_BATCH, D), 1.0),
        'state_delta': nrm(4, (N_A, DEC_BATCH, GDN_HEADS, GDN_DK, GDN_DV), 0.1),
        'state_conv': nrm(5, (N_A, DEC_BATCH, CONV_W - 1, CONV_DIM), 1.0),
        'cache_kv_latent': nrm(6, (DEC_BATCH, PAST_LEN, KV_LORA), 1.0),
        'cache_k_rope': nrm(7, (DEC_BATCH, PAST_LEN, QK_ROPE), 1.0),
        'w_mod': nrm(8, (DEPTH, D, 6 * D), 0.5 * D ** -0.5),
        'b_mod': nrm(9, (DEPTH, 6 * D), 0.02),
        'norm1_g': gain(10, (DEPTH, D)),
        'norm2_g': gain(11, (DEPTH, D)),
        'final_g': gain(12, (D,)),
        'gdn_w_in': nrm(13, (N_A, D, GDN_IN), D ** -0.5),
        'gdn_conv_w': nrm(14, (N_A, CONV_W, CONV_DIM), CONV_W ** -0.5),
        'gdn_a_log': jnp.log(jax.random.uniform(ks[15], (N_A, GDN_HEADS), f32, 1.0, 16.0)),
        'gdn_dt_bias': dt + jnp.log(-jnp.expm1(-dt)),
        'gdn_onorm_g': gain(17, (N_A, GDN_DV)),
        'gdn_w_o': nrm(18, (N_A, GDN_HEADS * GDN_DV, D), (GDN_HEADS * GDN_DV) ** -0.5),
        'kv_src_g': gain(19, (D,)),
        'mla_w_dkv': nrm(20, (D, KV_LORA + QK_ROPE), D ** -0.5),
        'mla_kv_norm_g': gain(21, (KV_LORA,)),
        'mla_w_uk': nrm(22, (KV_LORA, MLA_HEADS * QK_NOPE), KV_LORA ** -0.5),
        'mla_w_uv': nrm(23, (KV_LORA, MLA_HEADS * V_HEAD), KV_LORA ** -0.5),
        'mla_w_dq': nrm(24, (N_B, D, Q_LORA), D ** -0.5),
        'mla_q_norm_g': gain(25, (N_B, Q_LORA)),
        'mla_w_uq': nrm(26, (N_B, Q_LORA, MLA_HEADS * (QK_NOPE + QK_ROPE)), Q_LORA ** -0.5),
        'mla_w_o': nrm(27, (N_B, MLA_HEADS * V_HEAD, D), (MLA_HEADS * V_HEAD) ** -0.5),
        'peer_w_q': nrm(28, (DEPTH, D, PEER_HEADS * PK_DIM), D ** -0.5),
        'peer_sub_keys': nrm(29, (DEPTH, PEER_HEADS, 2, N_KEYS, PK_HALF), PK_HALF ** -0.5),
        'peer_u': nrm(30, (DEPTH, N_EXPERTS, D), D ** -0.5),
        'peer_v': nrm(31, (DEPTH, N_EXPERTS, D), PEER_HEADS ** -0.5),
    }


def reference(x_prompt, x_sample, c_prompt, c_sample, state_delta, state_conv, cache_kv_latent,
              cache_k_rope, w_mod, b_mod, norm1_g, norm2_g, final_g, gdn_w_in, gdn_conv_w, gdn_a_log,
              gdn_dt_bias, gdn_onorm_g, gdn_w_o, kv_src_g, mla_w_dkv, mla_kv_norm_g, mla_w_uk, mla_w_uv,
              mla_w_dq, mla_q_norm_g, mla_w_uq, mla_w_o, peer_w_q, peer_sub_keys, peer_u, peer_v):
    p = dict(w_mod=w_mod, b_mod=b_mod, norm1_g=norm1_g, norm2_g=norm2_g, final_g=final_g,
             gdn_w_in=gdn_w_in, gdn_conv_w=gdn_conv_w, gdn_a_log=gdn_a_log, gdn_dt_bias=gdn_dt_bias,
             gdn_onorm_g=gdn_onorm_g, gdn_w_o=gdn_w_o, kv_src_g=kv_src_g, mla_w_dkv=mla_w_dkv,
             mla_kv_norm_g=mla_kv_norm_g, mla_w_uk=mla_w_uk, mla_w_uv=mla_w_uv, mla_w_dq=mla_w_dq,
             mla_q_norm_g=mla_q_norm_g, mla_w_uq=mla_w_uq, mla_w_o=mla_w_o, peer_w_q=peer_w_q,
             peer_sub_keys=peer_sub_keys, peer_u=peer_u, peer_v=peer_v)

    bp, tp, _ = x_prompt.shape
    zero_bufs = jnp.zeros((N_A, bp, CONV_W - 1, CONV_DIM), x_prompt.dtype)
    zero_states = jnp.zeros((N_A, bp, GDN_HEADS, GDN_DK, GDN_DV), jnp.float32)
    no_c = jnp.zeros((bp, 0, KV_LORA), x_prompt.dtype)
    no_kr = jnp.zeros((bp, 0, QK_ROPE), x_prompt.dtype)
    y_prompt, p_delta, p_conv, p_lat, p_kr = trunk(x_prompt, c_prompt, jnp.arange(tp), zero_bufs,
                                                   zero_states, no_c, no_kr, p)

    past = cache_kv_latent.shape[1]
    ts = x_sample.shape[1]
    y_sample, s_delta, s_conv, s_lat, s_kr = trunk(x_sample, c_sample, past + jnp.arange(ts), state_conv,
                                                   state_delta, cache_kv_latent, cache_k_rope, p)
    return (y_prompt, y_sample, p_delta, p_conv, p_lat, p_kr, s_delta, s_conv, s_lat, s_kr)
```

```python
import functools
import math

import jax
import jax.numpy as jnp
from jax import lax
from jax.experimental import pallas as pl
from jax.experimental.pallas import tpu as pltpu

F32 = jnp.float32
BF16 = jnp.bfloat16
HIGHEST = lax.Precision.HIGHEST

D_MODEL = 1024
EPS = 1e-6
CHUNK = 64

GDN_HEADS = 8
GDN_DK = 128
GDN_DV = 128
CONV_W = 4
GDN_QK = GDN_HEADS * GDN_DK
CONV_DIM = GDN_HEADS * (2 * GDN_DK + GDN_DV)

MLA_HEADS = 8
QK_NOPE = 128
QK_ROPE = 64
V_HEAD = 128
KV_LORA = 256
Q_LORA = 384
ROPE_THETA = 10000.0
SM_SCALE = (QK_NOPE + QK_ROPE) ** -0.5
QK_PAD = 256

N_KEYS = 128
N_EXPERTS = N_KEYS * N_KEYS
PEER_HEADS = 8
PEER_TOPK = 16
PK_HALF = 128

LANES = 128
SUBLANES = 8
VMEM_LIMIT = 56 * 1024 * 1024


def _cparams(*sem):
    return pltpu.CompilerParams(dimension_semantics=sem, vmem_limit_bytes=VMEM_LIMIT)


def _silu(x):
    return x * jax.nn.sigmoid(x)


def _softplus(x):
    return jnp.maximum(x, 0.0) + jnp.log1p(jnp.exp(-jnp.abs(x)))


def _rms(x):
    return x * lax.rsqrt(jnp.mean(x * x, axis=-1, keepdims=True) + EPS)


def _dot(a, b, **kw):
    return jnp.dot(a, b, preferred_element_type=F32, **kw)


def _dot_nt(a, b, **kw):
    return lax.dot_general(a, b, (((1,), (1,)), ((), ())), preferred_element_type=F32, **kw)


def _dot_tn(a, b, **kw):
    return lax.dot_general(a, b, (((0,), (0,)), ((), ())), preferred_element_type=F32, **kw)


def _row_block(t, cap):
    b = min(t, cap)
    while t % b:
        b //= 2
    return b


def _mod_kernel(c_ref, w_ref, b_ref, o_ref):
    a = _silu(c_ref[...]).astype(BF16)
    o_ref[...] = _dot(a, w_ref[...].astype(BF16)) + b_ref[...]


def _modulation(c_pad, w_mod, b_mod):
    depth, d, n = w_mod.shape
    r = c_pad.shape[0]
    tn = 1024
    return pl.pallas_call(
        _mod_kernel,
        out_shape=jax.ShapeDtypeStruct((depth, r, n), F32),
        grid=(depth, n // tn),
        in_specs=[pl.BlockSpec((r, d), lambda l, j: (0, 0)),
                  pl.BlockSpec((None, d, tn), lambda l, j: (l, 0, j)),
                  pl.BlockSpec((None, 1, tn), lambda l, j: (l, 0, j))],
        out_specs=pl.BlockSpec((None, r, tn), lambda l, j: (l, 0, j)),
        compiler_params=_cparams("arbitrary", "arbitrary"),
        name="adaln_modulation",
    )(c_pad, w_mod, b_mod.reshape(depth, 1, n))


def _norm_mm_kernel(*refs, n_w, mod_off):
    x_ref, m_ref, g_ref = refs[:3]
    w_refs = refs[3:3 + n_w]
    o_refs = refs[3 + n_w:]
    y = _rms(x_ref[...]) * g_ref[...]
    if mod_off is not None:
        sh = m_ref[:, mod_off * D_MODEL:(mod_off + 1) * D_MODEL]
        sc = m_ref[:, (mod_off + 1) * D_MODEL:(mod_off + 2) * D_MODEL]
        y = y * (1.0 + sc) + sh
    yb = y.astype(BF16)
    for w_ref, o_ref in zip(w_refs, o_refs):
        o_ref[...] = _dot(yb, w_ref[...]).astype(o_ref.dtype)


def _norm_matmul(x, mods, g, weights, out_dtypes, *, mod_off, block_cap=512):
    b, t, d = x.shape
    tm = _row_block(t, block_cap)
    nb = t // tm
    n = b * t
    x2 = x.reshape(n, d)
    in_specs = [pl.BlockSpec((tm, d), lambda i: (i, 0)),
                pl.BlockSpec((None, 1, 6 * d), lambda i: (i // nb, 0, 0)),
                pl.BlockSpec((1, d), lambda i: (0, 0))]
    in_specs += [pl.BlockSpec(w.shape, lambda i: (0, 0)) for w in weights]
    outs = pl.pallas_call(
        functools.partial(_norm_mm_kernel, n_w=len(weights), mod_off=mod_off),
        out_shape=[jax.ShapeDtypeStruct((n, w.shape[1]), dt) for w, dt in zip(weights, out_dtypes)],
        grid=(n // tm,),
        in_specs=in_specs,
        out_specs=[pl.BlockSpec((tm, w.shape[1]), lambda i: (i, 0)) for w in weights],
        compiler_params=_cparams("parallel"),
        name="norm_matmul",
    )(x2, mods, g.reshape(1, d), *weights)
    return [o.reshape(b, t, o.shape[-1]) for o in outs]


def _gdn_kernel(qkv_ref, gate_ref, ab_ref, cbuf_ref, s0_ref, cw_ref, par_ref, og_ref,
                o_ref, s_ref, xp_ref, *, chunk):
    c = chunk
    step = pl.program_id(1)

    @pl.when(step == 0)
    def _():
        s_ref[...] = s0_ref[...]
        xp_ref[0:SUBLANES, :] = cbuf_ref[...]

    xp_ref[SUBLANES:SUBLANES + c, :] = qkv_ref[...]
    base = SUBLANES - (CONV_W - 1)
    conv = xp_ref[base:base + c, :] * cw_ref[0:1, :]
    for w in range(1, CONV_W):
        conv = conv + xp_ref[base + w:base + w + c, :] * cw_ref[w:w + 1, :]
    xp_ref[0:SUBLANES, :] = xp_ref[c:c + SUBLANES, :]
    act = _silu(conv)

    ab = ab_ref[...]
    g_all = -jnp.exp(par_ref[0:1, :]) * _softplus(ab + par_ref[1:2, :])
    beta_all = jax.nn.sigmoid(ab)

    row = lax.broadcasted_iota(jnp.int32, (c, c), 0)
    col = lax.broadcasted_iota(jnp.int32, (c, c), 1)
    causal = row >= col
    strict = row > col
    eye = (row == col).astype(F32)
    lower = causal.astype(F32)
    upper = (row <= col).astype(F32)
    r128 = lax.broadcasted_iota(jnp.int32, (LANES, LANES), 0)
    c128 = lax.broadcasted_iota(jnp.int32, (LANES, LANES), 1)
    eye128 = (r128 == c128).astype(F32)

    d_col_all = _dot(lower, g_all, precision=HIGHEST)
    g_t = _dot_nt(eye128, g_all, precision=HIGHEST)
    d_row_all = _dot(g_t, upper, precision=HIGHEST)

    n_double = int(math.log2(c)) - 1
    for h in range(GDN_HEADS):
        q = act[:, h * GDN_DK:(h + 1) * GDN_DK]
        k = act[:, GDN_QK + h * GDN_DK:GDN_QK + (h + 1) * GDN_DK]
        v = act[:, 2 * GDN_QK + h * GDN_DV:2 * GDN_QK + (h + 1) * GDN_DV]
        q = q * lax.rsqrt(jnp.sum(q * q, axis=-1, keepdims=True) + EPS) * (GDN_DK ** -0.5)
        k = k * lax.rsqrt(jnp.sum(k * k, axis=-1, keepdims=True) + EPS)
        beta = beta_all[:, GDN_HEADS + h:GDN_HEADS + h + 1]
        d_col = d_col_all[:, h:h + 1]
        d_row = d_row_all[h:h + 1, :]
        d_last = d_col_all[c - 1:c, h:h + 1]
        decay = jnp.exp(jnp.where(causal, d_col - d_row, -jnp.inf))
        e_col = jnp.exp(d_col)
        kb = k * beta
        kbf = k.astype(BF16)
        a_low = jnp.where(strict, _dot_nt(kb.astype(BF16), kbf) * decay, 0.0)
        m = -a_low
        p = eye + m
        for _ in range(n_double):
            m = _dot(m, m, precision=HIGHEST)
            p = p + _dot(p, m, precision=HIGHEST)
        rhs = jnp.concatenate([v * beta, kb * e_col], axis=1)
        sol = _dot(p, rhs, precision=HIGHEST)
        u = sol[:, :GDN_DV]
        w = sol[:, GDN_DV:]
        qk = _dot_nt(q.astype(BF16), kbf) * decay
        q_dec = (q * e_col).astype(BF16)
        k_dec = (k * jnp.exp(d_last - d_col)).astype(BF16)
        s = s_ref[h]
        sb = s.astype(BF16)
        v_new = u - _dot(w.astype(BF16), sb)
        vnb = v_new.astype(BF16)
        o = _dot(q_dec, sb) + _dot(qk.astype(BF16), vnb)
        s_ref[h] = s * jnp.exp(d_last) + _dot_tn(k_dec, vnb)
        gate = gate_ref[:, h * GDN_DV:(h + 1) * GDN_DV]
        o_ref[:, h * GDN_DV:(h + 1) * GDN_DV] = (_rms(o) * og_ref[...] * _silu(gate)).astype(o_ref.dtype)


def _gdn(qkv, gate, ab, conv_buf, s0, conv_w, a_log, dt_bias, onorm_g):
    b, t, _ = qkv.shape
    c = min(CHUNK, t)
    cbuf = jnp.pad(conv_buf, ((0, 0), (SUBLANES - (CONV_W - 1), 0), (0, 0)))
    cw = jnp.pad(conv_w, ((0, SUBLANES - CONV_W), (0, 0)))
    par = jnp.zeros((SUBLANES, LANES), F32)
    par = par.at[0, :GDN_HEADS].set(a_log).at[1, :GDN_HEADS].set(dt_bias)
    o, s = pl.pallas_call(
        functools.partial(_gdn_kernel, chunk=c),
        out_shape=[jax.ShapeDtypeStruct((b, t, GDN_HEADS * GDN_DV), BF16),
                   jax.ShapeDtypeStruct((b, GDN_HEADS, GDN_DK, GDN_DV), F32)],
        grid=(b, t // c),
        in_specs=[pl.BlockSpec((None, c, CONV_DIM), lambda i, j: (i, j, 0)),
                  pl.BlockSpec((None, c, GDN_HEADS * GDN_DV), lambda i, j: (i, j, 0)),
                  pl.BlockSpec((None, c, LANES), lambda i, j: (i, j, 0)),
                  pl.BlockSpec((None, SUBLANES, CONV_DIM), lambda i, j: (i, 0, 0)),
                  pl.BlockSpec((None, GDN_HEADS, GDN_DK, GDN_DV), lambda i, j: (i, 0, 0, 0)),
                  pl.BlockSpec((SUBLANES, CONV_DIM), lambda i, j: (0, 0)),
                  pl.BlockSpec((SUBLANES, LANES), lambda i, j: (0, 0)),
                  pl.BlockSpec((1, GDN_DV), lambda i, j: (0, 0))],
        out_specs=[pl.BlockSpec((None, c, GDN_HEADS * GDN_DV), lambda i, j: (i, j, 0)),
                   pl.BlockSpec((None, GDN_HEADS, GDN_DK, GDN_DV), lambda i, j: (i, 0, 0, 0))],
        scratch_shapes=[pltpu.VMEM((c + SUBLANES, CONV_DIM), F32)],
        compiler_params=_cparams("parallel", "arbitrary"),
        name="gated_delta",
    )(qkv, gate, ab, cbuf, s0, cw, par, onorm_g.reshape(1, GDN_DV))
    return o, s


def _out_res_kernel(o_ref, w_ref, x_ref, m_ref, g2_ref, y_ref, h2_ref):
    def mod(i):
        return m_ref[:, i * D_MODEL:(i + 1) * D_MODEL]

    y = x_ref[...] + mod(2) * _dot(o_ref[...], w_ref[...])
    y_ref[...] = y
    h2_ref[...] = (_rms(y) * g2_ref[...] * (1.0 + mod(4)) + mod(3)).astype(h2_ref.dtype)


def _out_residual(o, w, x, mods, norm2_g, *, block_cap=512):
    b, t, d = x.shape
    kdim = o.shape[-1]
    tm = _row_block(t, block_cap)
    nb = t // tm
    n = b * t
    y, h2 = pl.pallas_call(
        _out_res_kernel,
        out_shape=[jax.ShapeDtypeStruct((n, d), F32), jax.ShapeDtypeStruct((n, d), BF16)],
        grid=(n // tm,),
        in_specs=[pl.BlockSpec((tm, kdim), lambda i: (i, 0)),
                  pl.BlockSpec((kdim, d), lambda i: (0, 0)),
                  pl.BlockSpec((tm, d), lambda i: (i, 0)),
                  pl.BlockSpec((None, 1, 6 * d), lambda i: (i // nb, 0, 0)),
                  pl.BlockSpec((1, d), lambda i: (0, 0))],
        out_specs=[pl.BlockSpec((tm, d), lambda i: (i, 0)), pl.BlockSpec((tm, d), lambda i: (i, 0))],
        compiler_params=_cparams("parallel"),
        name="out_proj_residual",
    )(o.reshape(n, kdim), w, x.reshape(n, d), mods, norm2_g.reshape(1, d))
    return y.reshape(b, t, d), h2


_CAND = [(j1, j2) for j1 in range(PEER_TOPK) for j2 in range(PEER_TOPK) if (j1 + 1) * (j2 + 1) <= PEER_TOPK]
_CAND_PAD = -(-len(_CAND) // SUBLANES) * SUBLANES


def _top_values(s, n_top):
    vals = [jnp.max(s, axis=0, keepdims=True)]
    for _ in range(1, n_top):
        vals.append(jnp.max(jnp.where(s < vals[-1], s, -jnp.inf), axis=0, keepdims=True))
    return vals


def _rank_among(s, vals):
    rank = jnp.zeros(s.shape, F32)
    for v in vals:
        rank = rank + jnp.where(s < v, 1.0, 0.0)
    return rank


def _peer_route_kernel(h2_ref, wq_ref, keys_ref, rank2_ref, e2_ref, n1_ref, e1_ref, q_scr):
    q_scr[...] = _dot(h2_ref[...], wq_ref[...]).astype(BF16)
    tm = h2_ref.shape[0]

    def head(h, carry):
        off = pl.multiple_of(h * 2 * PK_HALF, 2 * PK_HALF)
        s1 = _dot_nt(keys_ref[h, 0], q_scr[:, pl.ds(off, PK_HALF)])
        s2 = _dot_nt(keys_ref[h, 1], q_scr[:, pl.ds(off + PK_HALF, PK_HALF)])
        v1 = _top_values(s1, PEER_TOPK)
        v2 = _top_values(s2, PEER_TOPK)
        rank1 = _rank_among(s1, v1)
        rank2 = _rank_among(s2, v2)
        rows = [v1[j1] + v2[j2] for j1, j2 in _CAND]
        rows += [jnp.full((1, tm), -jnp.inf, F32)] * (_CAND_PAD - len(_CAND))
        cand = jnp.concatenate(rows, axis=0)
        tau = _top_values(cand, PEER_TOPK)[-1]
        sel = jnp.where(cand >= tau, 1.0, 0.0)
        z = jnp.sum(sel * jnp.exp(cand - cand[0:1, :]), axis=0, keepdims=True)
        n1 = jnp.zeros(s1.shape, F32)
        off_c = 0
        for j1 in range(PEER_TOPK):
            cnt = PEER_TOPK // (j1 + 1)
            n_j = jnp.sum(sel[off_c:off_c + cnt, :], axis=0, keepdims=True)
            n1 = jnp.where(rank1 == float(j1), n_j, n1)
            off_c += cnt
        rank2_ref[h] = rank2.astype(rank2_ref.dtype)
        e2_ref[h] = jnp.exp(s2 - v2[0]).astype(e2_ref.dtype)
        n1_ref[h] = n1
        e1_ref[h] = jnp.exp(s1 - v1[0]) / z
        return carry

    lax.fori_loop(0, PEER_HEADS, head, 0)


def _peer_route(h2, w_q, sub_keys, *, tm):
    n, d = h2.shape
    shp = (PEER_HEADS, N_KEYS, n)
    blk = pl.BlockSpec((PEER_HEADS, N_KEYS, tm), lambda i: (0, 0, i))
    return pl.pallas_call(
        _peer_route_kernel,
        out_shape=[jax.ShapeDtypeStruct(shp, BF16), jax.ShapeDtypeStruct(shp, BF16),
                   jax.ShapeDtypeStruct(shp, F32), jax.ShapeDtypeStruct(shp, F32)],
        grid=(n // tm,),
        in_specs=[pl.BlockSpec((tm, d), lambda i: (i, 0)),
                  pl.BlockSpec(w_q.shape, lambda i: (0, 0)),
                  pl.BlockSpec(sub_keys.shape, lambda i: (0, 0, 0, 0))],
        out_specs=[blk, blk, blk, blk],
        scratch_shapes=[pltpu.VMEM((tm, w_q.shape[1]), BF16)],
        compiler_params=_cparams("parallel"),
        name="peer_route",
    )(h2, w_q, sub_keys)


def _peer_dense_kernel(h2_ref, u_ref, vt_ref, rank2_ref, e2_ref, n1_ref, e1_ref, x_ref, m_ref, fg_ref,
                       y_ref, acc_ref, *, eb, final_norm):
    j = pl.program_id(1)
    rows_per_step = eb // N_KEYS

    @pl.when(j == 0)
    def _():
        acc_ref[...] = jnp.zeros_like(acc_ref)

    act = jax.nn.gelu(_dot_nt(u_ref[...], h2_ref[...]))
    pieces = []
    for r in range(rows_per_step):
        i1 = j * rows_per_step + r
        w = None
        for h in range(PEER_HEADS):
            n_row = n1_ref[h, pl.ds(i1, 1), :].astype(BF16)
            e_row = e1_ref[h, pl.ds(i1, 1), :].astype(BF16)
            term = jnp.where(rank2_ref[h] < n_row, e2_ref[h], jnp.zeros((), BF16)) * e_row
            w = term if w is None else w + term
        pieces.append((act[r * N_KEYS:(r + 1) * N_KEYS, :] * w.astype(F32)).astype(BF16))
    wa = jnp.concatenate(pieces, axis=0) if len(pieces) > 1 else pieces[0]
    acc_ref[...] += _dot(vt_ref[...], wa)

    @pl.when(j == pl.num_programs(1) - 1)
    def _():
        nseq, _, _ = m_ref.shape
        tm, d = x_ref.shape
        g2 = m_ref[:, :, 5 * D_MODEL:6 * D_MODEL]
        out = acc_ref[...].T.reshape(nseq, tm // nseq, d)
        y = (x_ref[...].reshape(nseq, tm // nseq, d) + g2 * out).reshape(tm, d)
        if final_norm:
            y = _rms(y) * fg_ref[...]
        y_ref[...] = y


def _peer_dense(h2, route, u_b, vt_b, x, mods, final_g, *, tm, eb, final_norm):
    b, t, d = x.shape
    n = b * t
    nseq = max(1, tm // t)
    rank2, e2, n1, e1 = route
    tok = pl.BlockSpec((PEER_HEADS, N_KEYS, tm), lambda i, j: (0, 0, i))
    y = pl.pallas_call(
        functools.partial(_peer_dense_kernel, eb=eb, final_norm=final_norm),
        out_shape=jax.ShapeDtypeStruct((n, d), F32),
        grid=(n // tm, N_EXPERTS // eb),
        in_specs=[pl.BlockSpec((tm, d), lambda i, j: (i, 0)),
                  pl.BlockSpec((eb, d), lambda i, j: (j, 0)),
                  pl.BlockSpec((d, eb), lambda i, j: (0, j)),
                  tok, tok, tok, tok,
                  pl.BlockSpec((tm, d), lambda i, j: (i, 0)),
                  pl.BlockSpec((nseq, 1, 6 * d), lambda i, j: ((i * tm) // t // nseq, 0, 0)),
                  pl.BlockSpec((1, d), lambda i, j: (0, 0))],
        out_specs=pl.BlockSpec((tm, d), lambda i, j: (i, 0)),
        scratch_shapes=[pltpu.VMEM((d, tm), F32)],
        compiler_params=_cparams("parallel", "arbitrary"),
        name="peer_experts",
    )(h2, u_b, vt_b, rank2, e2, n1, e1, x.reshape(n, d), mods, final_g.reshape(1, d))
    return y.reshape(b, t, d)


def _peer(h2, x, mods, w_q, sub_keys, u_b, vt_b, final_g, *, final_norm):
    n = h2.shape[0]
    tm = _row_block(n, 512)
    route = _peer_route(h2, w_q, sub_keys, tm=min(tm, 256))
    return _peer_dense(h2, route, u_b, vt_b, x, mods, final_g, tm=tm, eb=512, final_norm=final_norm)


def _latent_kernel(x_ref, g_ref, wc_ref, wr_ref, cs_ref, kvg_ref, c_ref, kr_ref):
    xn = (_rms(x_ref[...]) * g_ref[...]).astype(BF16)
    c_ref[...] = _rms(_dot(xn, wc_ref[...])) * kvg_ref[...]
    p = _dot(xn, wr_ref[...]) * cs_ref[...]
    kr_ref[...] = p[:, :QK_ROPE] + p[:, QK_ROPE:]


def _shared_latent(x, kv_src_g, w_c, w_r2, cs_k, kv_norm_g, *, block_cap=512):
    b, t, d = x.shape
    tm = _row_block(t, block_cap)
    nb = t // tm
    n = b * t
    c_kv, k_r = pl.pallas_call(
        _latent_kernel,
        out_shape=[jax.ShapeDtypeStruct((n, KV_LORA), F32), jax.ShapeDtypeStruct((n, QK_ROPE), F32)],
        grid=(n // tm,),
        in_specs=[pl.BlockSpec((tm, d), lambda i: (i, 0)),
                  pl.BlockSpec((1, d), lambda i: (0, 0)),
                  pl.BlockSpec(w_c.shape, lambda i: (0, 0)),
                  pl.BlockSpec(w_r2.shape, lambda i: (0, 0)),
                  pl.BlockSpec((tm, 2 * QK_ROPE), lambda i: (i % nb, 0)),
                  pl.BlockSpec((1, KV_LORA), lambda i: (0, 0))],
        out_specs=[pl.BlockSpec((tm, KV_LORA), lambda i: (i, 0)), pl.BlockSpec((tm, QK_ROPE), lambda i: (i, 0))],
        compiler_params=_cparams("parallel"),
        name="mla_shared_latent",
    )(x.reshape(n, d), kv_src_g.reshape(1, d), w_c, w_r2, cs_k, kv_norm_g.reshape(1, KV_LORA))
    return c_kv.reshape(b, t, KV_LORA), k_r.reshape(b, t, QK_ROPE)


def _query_kernel(x_ref, m_ref, g_ref, wdq_ref, qg_ref, wq_ref, wqs_ref, c_ref, s_ref, q_ref):
    y = _rms(x_ref[...]) * g_ref[...]
    y = y * (1.0 + m_ref[:, D_MODEL:2 * D_MODEL]) + m_ref[:, 0:D_MODEL]
    cq = (_rms(_dot(y.astype(BF16), wdq_ref[...])) * qg_ref[...]).astype(BF16)
    q = _dot(cq, wq_ref[...])
    qs = _dot(cq, wqs_ref[...])
    cos = c_ref[...]
    sin = s_ref[...]
    for h in range(MLA_HEADS):
        sl = slice(h * QK_PAD, (h + 1) * QK_PAD)
        q_ref[:, sl] = (q[:, sl] * cos + qs[:, sl] * sin).astype(q_ref.dtype)


def _mla_queries(x, mods, norm1_g, w_dq, q_norm_g, w_q, w_qs, cos_q, sin_q, *, block_cap=512):
    b, t, d = x.shape
    tm = _row_block(t, block_cap)
    nb = t // tm
    n = b * t
    q = pl.pallas_call(
        _query_kernel,
        out_shape=jax.ShapeDtypeStruct((n, MLA_HEADS * QK_PAD), BF16),
        grid=(n // tm,),
        in_specs=[pl.BlockSpec((tm, d), lambda i: (i, 0)),
                  pl.BlockSpec((None, 1, 6 * d), lambda i: (i // nb, 0, 0)),
                  pl.BlockSpec((1, d), lambda i: (0, 0)),
                  pl.BlockSpec(w_dq.shape, lambda i: (0, 0)),
                  pl.BlockSpec((1, Q_LORA), lambda i: (0, 0)),
                  pl.BlockSpec(w_q.shape, lambda i: (0, 0)),
                  pl.BlockSpec(w_qs.shape, lambda i: (0, 0)),
                  pl.BlockSpec((tm, QK_PAD), lambda i: (i % nb, 0)),
                  pl.BlockSpec((tm, QK_PAD), lambda i: (i % nb, 0))],
        out_specs=pl.BlockSpec((tm, MLA_HEADS * QK_PAD), lambda i: (i, 0)),
        compiler_params=_cparams("parallel"),
        name="mla_queries",
    )(x.reshape(n, d), mods, norm1_g.reshape(1, d), w_dq, q_norm_g.reshape(1, Q_LORA), w_q, w_qs, cos_q, sin_q)
    return q.reshape(b, t, MLA_HEADS * QK_PAD)


def _kv_kernel(c_ref, kr_ref, wuk_ref, wuv_ref, k_ref, v_ref):
    cb = c_ref[...].astype(BF16)
    kn = _dot(cb, wuk_ref[...])
    v_ref[...] = _dot(cb, wuv_ref[...]).astype(v_ref.dtype)
    tm = c_ref.shape[0]
    tail = jnp.concatenate([kr_ref[...], jnp.zeros((tm, QK_PAD - QK_NOPE - QK_ROPE), F32)], axis=1)
    tail = tail.astype(k_ref.dtype)
    for h in range(MLA_HEADS):
        k_ref[:, h * QK_PAD:h * QK_PAD + QK_NOPE] = kn[:, h * QK_NOPE:(h + 1) * QK_NOPE].astype(k_ref.dtype)
        k_ref[:, h * QK_PAD + QK_NOPE:(h + 1) * QK_PAD] = tail


def _mla_keys_values(c_all, kr_all, w_uk, w_uv, *, block_cap=512):
    b, tk, _ = c_all.shape
    tm = _row_block(tk, block_cap)
    n = b * tk
    k, v = pl.pallas_call(
        _kv_kernel,
        out_shape=[jax.ShapeDtypeStruct((n, MLA_HEADS * QK_PAD), BF16),
                   jax.ShapeDtypeStruct((n, MLA_HEADS * V_HEAD), BF16)],
        grid=(n // tm,),
        in_specs=[pl.BlockSpec((tm, KV_LORA), lambda i: (i, 0)),
                  pl.BlockSpec((tm, QK_ROPE), lambda i: (i, 0)),
                  pl.BlockSpec(w_uk.shape, lambda i: (0, 0)),
                  pl.BlockSpec(w_uv.shape, lambda i: (0, 0))],
        out_specs=[pl.BlockSpec((tm, MLA_HEADS * QK_PAD), lambda i: (i, 0)),
                   pl.BlockSpec((tm, MLA_HEADS * V_HEAD), lambda i: (i, 0))],
        compiler_params=_cparams("parallel"),
        name="mla_keys_values",
    )(c_all.reshape(n, KV_LORA), kr_all.reshape(n, QK_ROPE), w_uk, w_uv)
    return k.reshape(b, tk, -1), v.reshape(b, tk, -1)


def _attn_kernel(q_ref, k_ref, v_ref, o_ref, *, tk, pos0, n_keys):
    tq = q_ref.shape[0]
    q = q_ref[...]
    q_pos0 = pos0 + pl.program_id(2) * tq
    q_chunk = (q_pos0 + lax.broadcasted_iota(jnp.int32, (tq, 1), 0)) // CHUNK
    k_end = jnp.minimum(((q_pos0 + tq - 1) // CHUNK + 1) * CHUNK, n_keys)
    n_blocks = (k_end + tk - 1) // tk

    def body(kb, carry):
        m, l, acc = carry
        start = pl.multiple_of(kb * tk, tk)
        s = _dot_nt(q, k_ref[pl.ds(start, tk), :]) * SM_SCALE
        k_pos = start + lax.broadcasted_iota(jnp.int32, (1, tk), 1)
        visible = (k_pos // CHUNK <= q_chunk) & (k_pos < n_keys)
        s = jnp.where(visible, s, -jnp.inf)
        m_new = jnp.maximum(m, jnp.max(s, axis=-1, keepdims=True))
        p = jnp.exp(s - m_new)
        alpha = jnp.exp(m - m_new)
        l = alpha * l + jnp.sum(p, axis=-1, keepdims=True)
        acc = alpha * acc + _dot(p.astype(BF16), v_ref[pl.ds(start, tk), :])
        return m_new, l, acc

    init = (jnp.full((tq, 1), -jnp.inf, F32), jnp.zeros((tq, 1), F32), jnp.zeros((tq, V_HEAD), F32))
    _, l, acc = lax.fori_loop(0, n_blocks, body, init)
    o_ref[...] = (acc / l).astype(o_ref.dtype)


def _attention(q, k, v, *, pos0, n_keys):
    b, t, _ = q.shape
    tkeys = k.shape[1]
    tq = _row_block(t, 256)
    tk = _row_block(tkeys, 512)
    return pl.pallas_call(
        functools.partial(_attn_kernel, tk=tk, pos0=pos0, n_keys=n_keys),
        out_shape=jax.ShapeDtypeStruct((b, t, MLA_HEADS * V_HEAD), BF16),
        grid=(b, MLA_HEADS, t // tq),
        in_specs=[pl.BlockSpec((None, tq, QK_PAD), lambda bi, h, i: (bi, i, h)),
                  pl.BlockSpec((None, tkeys, QK_PAD), lambda bi, h, i: (bi, 0, h)),
                  pl.BlockSpec((None, tkeys, V_HEAD), lambda bi, h, i: (bi, 0, h))],
        out_specs=pl.BlockSpec((None, tq, V_HEAD), lambda bi, h, i: (bi, i, h)),
        compiler_params=_cparams("parallel", "parallel", "arbitrary"),
        name="mla_attention",
    )(q, k, v)


def _rope_tables(pos):
    half = QK_ROPE // 2
    inv = ROPE_THETA ** (-jnp.arange(half, dtype=F32) / half)
    ang = pos.astype(F32)[:, None] * inv[None, :]
    cos, sin = jnp.cos(ang), jnp.sin(ang)
    t = pos.shape[0]
    cs_k = jnp.concatenate([cos, cos, -sin, sin], axis=1)
    one = jnp.ones((t, QK_NOPE), F32)
    zero_n = jnp.zeros((t, QK_NOPE), F32)
    zero_p = jnp.zeros((t, QK_PAD - QK_NOPE - QK_ROPE), F32)
    cos_q = jnp.concatenate([one, cos, cos, zero_p], axis=1)
    sin_q = jnp.concatenate([zero_n, -sin, sin, zero_p], axis=1)
    return cs_k, cos_q, sin_q


def _swap_halves(w):
    half = QK_ROPE // 2
    return jnp.concatenate([w[..., half:], w[..., :half]], axis=-1)


def _prep_weights(p):
    o1 = CONV_DIM + GDN_HEADS * GDN_DV
    w = {}
    w_in = p['gdn_w_in']
    w['gdn_qkv'] = w_in[:, :, :CONV_DIM].astype(BF16)
    w['gdn_gate'] = w_in[:, :, CONV_DIM:o1].astype(BF16)
    w['gdn_ab'] = jnp.pad(w_in[:, :, o1:], ((0, 0), (0, 0), (0, LANES - 2 * GDN_HEADS))).astype(BF16)
    w['gdn_o'] = p['gdn_w_o'].astype(BF16)
    w_dkv = p['mla_w_dkv']
    w['dkv_c'] = w_dkv[:, :KV_LORA].astype(BF16)
    rope_cols = w_dkv[:, KV_LORA:]
    w['dkv_r2'] = jnp.concatenate([rope_cols, _swap_halves(rope_cols)], axis=1).astype(BF16)
    w['uk'] = p['mla_w_uk'].astype(BF16)
    w['uv'] = p['mla_w_uv'].astype(BF16)
    w['dq'] = p['mla_w_dq'].astype(BF16)
    n_b = p['mla_w_uq'].shape[0]
    uq = p['mla_w_uq'].reshape(n_b, Q_LORA, MLA_HEADS, QK_NOPE + QK_ROPE)
    pad = jnp.zeros((n_b, Q_LORA, MLA_HEADS, QK_PAD - QK_NOPE - QK_ROPE), F32)
    w['uq'] = jnp.concatenate([uq, pad], axis=-1).reshape(n_b, Q_LORA, MLA_HEADS * QK_PAD).astype(BF16)
    uq_s = jnp.concatenate([jnp.zeros_like(uq[..., :QK_NOPE]), _swap_halves(uq[..., QK_NOPE:]), pad], axis=-1)
    w['uq_swap'] = uq_s.reshape(n_b, Q_LORA, MLA_HEADS * QK_PAD).astype(BF16)
    w['mla_o'] = p['mla_w_o'].astype(BF16)
    w['peer_q'] = p['peer_w_q'].astype(BF16)
    w['peer_keys'] = p['peer_sub_keys'].astype(BF16)
    w['peer_u'] = p['peer_u'].astype(BF16)
    w['peer_vt'] = jnp.swapaxes(p['peer_v'], 1, 2).astype(BF16)
    return w


def _trunk(x, mods, pos0, conv_bufs, delta_states, past_c, past_kr, p, w):
    b, t, d = x.shape
    depth = mods.shape[0]
    n_a = p['gdn_w_in'].shape[0]
    new_states, new_bufs = [], []
    c_kv = k_r = k_cat = v_all = None
    n_keys = past_c.shape[1] + t
    for l in range(depth):
        m = mods[l]
        if l < n_a:
            qkv, gate, ab = _norm_matmul(x, m, p['norm1_g'][l], [w['gdn_qkv'][l], w['gdn_gate'][l], w['gdn_ab'][l]],
                                         [F32, F32, F32], mod_off=0)
            o, s = _gdn(qkv, gate, ab, conv_bufs[l], delta_states[l], p['gdn_conv_w'][l], p['gdn_a_log'][l],
                        p['gdn_dt_bias'][l], p['gdn_onorm_g'][l])
            new_bufs.append(qkv[:, t - (CONV_W - 1):, :])
            new_states.append(s)
            w_o = w['gdn_o'][l]
        else:
            j = l - n_a
            cs_k, cos_q, sin_q = _rope_tables(pos0 + jnp.arange(t))
            if j == 0:
                c_kv, k_r = _shared_latent(x, p['kv_src_g'], w['dkv_c'], w['dkv_r2'], cs_k, p['mla_kv_norm_g'])
                c_all = jnp.concatenate([past_c, c_kv], axis=1)
                kr_all = jnp.concatenate([past_kr, k_r], axis=1)
                pad = -n_keys % LANES
                c_all = jnp.pad(c_all, ((0, 0), (0, pad), (0, 0)))
                kr_all = jnp.pad(kr_all, ((0, 0), (0, pad), (0, 0)))
                k_cat, v_all = _mla_keys_values(c_all, kr_all, w['uk'], w['uv'])
            q = _mla_queries(x, m, p['norm1_g'][l], w['dq'][j], p['mla_q_norm_g'][j], w['uq'][j], w['uq_swap'][j],
                             cos_q, sin_q)
            o = _attention(q, k_cat, v_all, pos0=pos0, n_keys=n_keys)
            w_o = w['mla_o'][j]
        x, h2 = _out_residual(o, w_o, x, m, p['norm2_g'][l])
        x = _peer(h2, x, m, w['peer_q'][l], w['peer_keys'][l], w['peer_u'][l], w['peer_vt'][l], p['final_g'],
                  final_norm=(l == depth - 1))
    return x, jnp.stack(new_states), jnp.stack(new_bufs), c_kv, k_r


def kernel(x_prompt, x_sample, c_prompt, c_sample, state_delta, state_conv, cache_kv_latent, cache_k_rope, w_mod, b_mod, norm1_g, norm2_g, final_g, gdn_w_in, gdn_conv_w, gdn_a_log, gdn_dt_bias, gdn_onorm_g, gdn_w_o, kv_src_g, mla_w_dkv, mla_kv_norm_g, mla_w_uk, mla_w_uv, mla_w_dq, mla_q_norm_g, mla_w_uq, mla_w_o, peer_w_q, peer_sub_keys, peer_u, peer_v):
    p = dict(norm1_g=norm1_g, norm2_g=norm2_g, final_g=final_g, gdn_w_in=gdn_w_in, gdn_conv_w=gdn_conv_w,
             gdn_a_log=gdn_a_log, gdn_dt_bias=gdn_dt_bias, gdn_onorm_g=gdn_onorm_g, gdn_w_o=gdn_w_o,
             kv_src_g=kv_src_g, mla_w_dkv=mla_w_dkv, mla_kv_norm_g=mla_kv_norm_g, mla_w_uk=mla_w_uk,
             mla_w_uv=mla_w_uv, mla_w_dq=mla_w_dq, mla_q_norm_g=mla_q_norm_g, mla_w_uq=mla_w_uq, mla_w_o=mla_w_o,
             peer_w_q=peer_w_q, peer_sub_keys=peer_sub_keys, peer_u=peer_u, peer_v=peer_v)
    w = _prep_weights(p)
    bp, tp, d = x_prompt.shape
    bs, ts, _ = x_sample.shape
    n_a = gdn_w_in.shape[0]

    rows = -(-(bp + bs) // SUBLANES) * SUBLANES
    c_all = jnp.pad(jnp.concatenate([c_prompt, c_sample], axis=0), ((0, rows - bp - bs), (0, 0)))
    mods = _modulation(c_all, w_mod, b_mod)
    mods_p = mods[:, :bp, None, :]
    mods_s = mods[:, bp:bp + bs, None, :]

    zero_bufs = jnp.zeros((n_a, bp, CONV_W - 1, CONV_DIM), F32)
    zero_states = jnp.zeros((n_a, bp, GDN_HEADS, GDN_DK, GDN_DV), F32)
    no_c = jnp.zeros((bp, 0, KV_LORA), F32)
    no_kr = jnp.zeros((bp, 0, QK_ROPE), F32)
    y_p, p_delta, p_conv, p_lat, p_kr = _trunk(x_prompt, mods_p, 0, zero_bufs, zero_states, no_c, no_kr, p, w)
    past = cache_kv_latent.shape[1]
    y_s, s_delta, s_conv, s_lat, s_kr = _trunk(x_sample, mods_s, past, state_conv, state_delta, cache_kv_latent,
                                               cache_k_rope, p, w)
    return (y_p, y_s, p_delta, p_conv, p_lat, p_kr, s_delta, s_conv, s_lat, s_kr)
```

```python
import functools
import math

import jax
import jax.numpy as jnp
from jax import lax
from jax.experimental import pallas as pl
from jax.experimental.pallas import tpu as pltpu

F32 = jnp.float32
BF16 = jnp.bfloat16
HIGHEST = lax.Precision.HIGHEST

D_MODEL = 1024
EPS = 1e-6
CHUNK = 64

GDN_HEADS = 8
GDN_DK = 128
GDN_DV = 128
CONV_W = 4
GDN_QK = GDN_HEADS * GDN_DK
CONV_DIM = GDN_HEADS * (2 * GDN_DK + GDN_DV)

MLA_HEADS = 8
QK_NOPE = 128
QK_ROPE = 64
V_HEAD = 128
KV_LORA = 256
Q_LORA = 384
ROPE_THETA = 10000.0
SM_SCALE = (QK_NOPE + QK_ROPE) ** -0.5
QK_PAD = 256

N_KEYS = 128
N_EXPERTS = N_KEYS * N_KEYS
PEER_HEADS = 8
PEER_TOPK = 16
PK_HALF = 128

LANES = 128
SUBLANES = 8
BF16_ROWS = 16
VMEM_LIMIT = 56 * 1024 * 1024


def _cparams(*sem):
    return pltpu.CompilerParams(dimension_semantics=sem, vmem_limit_bytes=VMEM_LIMIT)


def _silu(x):
    return x * jax.nn.sigmoid(x)


def _softplus(x):
    return jnp.maximum(x, 0.0) + jnp.log1p(jnp.exp(-jnp.abs(x)))


def _rms(x):
    return x * lax.rsqrt(jnp.mean(x * x, axis=-1, keepdims=True) + EPS)


def _dot(a, b, **kw):
    return jnp.dot(a, b, preferred_element_type=F32, **kw)


def _dot_nt(a, b, **kw):
    return lax.dot_general(a, b, (((1,), (1,)), ((), ())), preferred_element_type=F32, **kw)


def _dot_tn(a, b, **kw):
    return lax.dot_general(a, b, (((0,), (0,)), ((), ())), preferred_element_type=F32, **kw)


def _split_bf16(a):
    hi = a.astype(BF16)
    return hi, (a - hi.astype(F32)).astype(BF16)


def _dot3(a, b):
    ah, al = _split_bf16(a)
    bh, bl = _split_bf16(b)
    return _dot(jnp.concatenate([ah, ah, al], axis=1), jnp.concatenate([bh, bl, bh], axis=0))


def _row_block(t, cap):
    b = min(t, cap)
    while t % b:
        b //= 2
    return b


def _mod_kernel(c_ref, w_ref, b_ref, o_ref):
    a = _silu(c_ref[...]).astype(BF16)
    o_ref[...] = _dot(a, w_ref[...].astype(BF16)) + b_ref[...]


def _modulation(c_pad, w_mod, b_mod):
    depth, d, n = w_mod.shape
    r = c_pad.shape[0]
    tn = 1024
    return pl.pallas_call(
        _mod_kernel,
        out_shape=jax.ShapeDtypeStruct((depth, r, n), F32),
        grid=(depth, n // tn),
        in_specs=[pl.BlockSpec((r, d), lambda l, j: (0, 0)),
                  pl.BlockSpec((None, d, tn), lambda l, j: (l, 0, j)),
                  pl.BlockSpec((None, 1, tn), lambda l, j: (l, 0, j))],
        out_specs=pl.BlockSpec((None, r, tn), lambda l, j: (l, 0, j)),
        compiler_params=_cparams("arbitrary", "arbitrary"),
        name="adaln_modulation",
    )(c_pad, w_mod, b_mod.reshape(depth, 1, n))


def _norm_mm_kernel(*refs, n_w, mod_off):
    x_ref, m_ref, g_ref = refs[:3]
    w_refs = refs[3:3 + n_w]
    o_refs = refs[3 + n_w:]
    y = _rms(x_ref[...]) * g_ref[...]
    if mod_off is not None:
        sh = m_ref[:, mod_off * D_MODEL:(mod_off + 1) * D_MODEL]
        sc = m_ref[:, (mod_off + 1) * D_MODEL:(mod_off + 2) * D_MODEL]
        y = y * (1.0 + sc) + sh
    yb = y.astype(BF16)
    for w_ref, o_ref in zip(w_refs, o_refs):
        o_ref[...] = _dot(yb, w_ref[...]).astype(o_ref.dtype)


def _norm_matmul(x, mods, g, weights, out_dtypes, *, mod_off, block_cap=512):
    b, t, d = x.shape
    tm = _row_block(t, block_cap)
    nb = t // tm
    n = b * t
    x2 = x.reshape(n, d)
    in_specs = [pl.BlockSpec((tm, d), lambda i: (i, 0)),
                pl.BlockSpec((None, 1, 6 * d), lambda i: (i // nb, 0, 0)),
                pl.BlockSpec((1, d), lambda i: (0, 0))]
    in_specs += [pl.BlockSpec(w.shape, lambda i: (0, 0)) for w in weights]
    outs = pl.pallas_call(
        functools.partial(_norm_mm_kernel, n_w=len(weights), mod_off=mod_off),
        out_shape=[jax.ShapeDtypeStruct((n, w.shape[1]), dt) for w, dt in zip(weights, out_dtypes)],
        grid=(n // tm,),
        in_specs=in_specs,
        out_specs=[pl.BlockSpec((tm, w.shape[1]), lambda i: (i, 0)) for w in weights],
        compiler_params=_cparams("parallel"),
        name="norm_matmul",
    )(x2, mods, g.reshape(1, d), *weights)
    return [o.reshape(b, t, o.shape[-1]) for o in outs]


def _gdn_kernel(qkv_ref, gate_ref, ab_ref, cbuf_ref, s0_ref, cw_ref, par_ref, og_ref,
                o_ref, s_ref, xp_ref, *, chunk):
    c = chunk
    step = pl.program_id(1)

    @pl.when(step == 0)
    def _():
        s_ref[...] = s0_ref[...]
        xp_ref[0:SUBLANES, :] = cbuf_ref[...]

    xp_ref[SUBLANES:SUBLANES + c, :] = qkv_ref[...]
    base = SUBLANES - (CONV_W - 1)
    conv = xp_ref[base:base + c, :] * cw_ref[0:1, :]
    for w in range(1, CONV_W):
        conv = conv + xp_ref[base + w:base + w + c, :] * cw_ref[w:w + 1, :]
    xp_ref[0:SUBLANES, :] = xp_ref[c:c + SUBLANES, :]
    act = _silu(conv)

    ab = ab_ref[...]
    g_all = -jnp.exp(par_ref[0:1, :]) * _softplus(ab + par_ref[1:2, :])
    beta_all = jax.nn.sigmoid(ab)

    row = lax.broadcasted_iota(jnp.int32, (c, c), 0)
    col = lax.broadcasted_iota(jnp.int32, (c, c), 1)
    causal = row >= col
    strict = row > col
    eye = (row == col).astype(F32)
    lower = causal.astype(F32)
    upper = (row <= col).astype(F32)
    r128 = lax.broadcasted_iota(jnp.int32, (LANES, LANES), 0)
    c128 = lax.broadcasted_iota(jnp.int32, (LANES, LANES), 1)
    eye128 = (r128 == c128).astype(F32)

    d_col_all = _dot(lower, g_all, precision=HIGHEST)
    g_t = _dot_nt(eye128, g_all, precision=HIGHEST)
    d_row_all = _dot(g_t, upper, precision=HIGHEST)

    n_double = int(math.log2(c)) - 1
    keep_s = lax.broadcasted_iota(jnp.int32, (c, 2 * c), 1) < c
    heads = range(GDN_HEADS)
    q, k, v, beta, d_col, d_last, decay, e_col, kb, kq = ([None] * GDN_HEADS for _ in range(10))
    for h in heads:
        qh = act[:, h * GDN_DK:(h + 1) * GDN_DK]
        kh = act[:, GDN_QK + h * GDN_DK:GDN_QK + (h + 1) * GDN_DK]
        v[h] = act[:, 2 * GDN_QK + h * GDN_DV:2 * GDN_QK + (h + 1) * GDN_DV]
        q[h] = qh * lax.rsqrt(jnp.sum(qh * qh, axis=-1, keepdims=True) + EPS) * (GDN_DK ** -0.5)
        k[h] = kh * lax.rsqrt(jnp.sum(kh * kh, axis=-1, keepdims=True) + EPS)
        beta[h] = beta_all[:, GDN_HEADS + h:GDN_HEADS + h + 1]
        d_col[h] = d_col_all[:, h:h + 1]
        d_last[h] = d_col_all[c - 1:c, h:h + 1]
        decay[h] = jnp.exp(jnp.where(causal, d_col[h] - d_row_all[h:h + 1, :], -jnp.inf))
        e_col[h] = jnp.exp(d_col[h])
        kb[h] = k[h] * beta[h]
    for h in heads:
        kq[h] = _dot_nt(jnp.concatenate([kb[h], q[h]], axis=0).astype(BF16), k[h].astype(BF16))
    m = [-jnp.where(strict, kq[h][:c] * decay[h], 0.0) for h in heads]
    x = [jnp.concatenate([eye, m[h]], axis=1) for h in heads]
    for _ in range(n_double):
        for h in heads:
            x[h] = jnp.where(keep_s, x[h], 0.0) + _dot3(m[h], x[h])
            m[h] = x[h][:, c:]
    p = [x[h][:, :c] + _dot3(m[h], x[h][:, :c]) for h in heads]
    sol = [_dot3(p[h], jnp.concatenate([v[h] * beta[h], kb[h] * e_col[h]], axis=1)) for h in heads]
    s = [s_ref[h] for h in heads]
    ws = [_dot(jnp.concatenate([sol[h][:, GDN_DV:], q[h] * e_col[h]], axis=0).astype(BF16), s[h].astype(BF16))
          for h in heads]
    vnb = [(sol[h][:, :GDN_DV] - ws[h][:c]).astype(BF16) for h in heads]
    o = [ws[h][c:] + _dot((kq[h][c:] * decay[h]).astype(BF16), vnb[h]) for h in heads]
    for h in heads:
        k_dec = (k[h] * jnp.exp(d_last[h] - d_col[h])).astype(BF16)
        s_ref[h] = s[h] * jnp.exp(d_last[h]) + _dot_tn(k_dec, vnb[h])
    for h in heads:
        gate = gate_ref[:, h * GDN_DV:(h + 1) * GDN_DV]
        o_ref[:, h * GDN_DV:(h + 1) * GDN_DV] = (_rms(o[h]) * og_ref[...] * _silu(gate)).astype(o_ref.dtype)


def _gdn(qkv, gate, ab, conv_buf, s0, conv_w, a_log, dt_bias, onorm_g):
    b, t, _ = qkv.shape
    c = min(CHUNK, t)
    cbuf = jnp.pad(conv_buf, ((0, 0), (SUBLANES - (CONV_W - 1), 0), (0, 0)))
    cw = jnp.pad(conv_w, ((0, SUBLANES - CONV_W), (0, 0)))
    par = jnp.zeros((SUBLANES, LANES), F32)
    par = par.at[0, :GDN_HEADS].set(a_log).at[1, :GDN_HEADS].set(dt_bias)
    o, s = pl.pallas_call(
        functools.partial(_gdn_kernel, chunk=c),
        out_shape=[jax.ShapeDtypeStruct((b, t, GDN_HEADS * GDN_DV), BF16),
                   jax.ShapeDtypeStruct((b, GDN_HEADS, GDN_DK, GDN_DV), F32)],
        grid=(b, t // c),
        in_specs=[pl.BlockSpec((None, c, CONV_DIM), lambda i, j: (i, j, 0)),
                  pl.BlockSpec((None, c, GDN_HEADS * GDN_DV), lambda i, j: (i, j, 0)),
                  pl.BlockSpec((None, c, LANES), lambda i, j: (i, j, 0)),
                  pl.BlockSpec((None, SUBLANES, CONV_DIM), lambda i, j: (i, 0, 0)),
                  pl.BlockSpec((None, GDN_HEADS, GDN_DK, GDN_DV), lambda i, j: (i, 0, 0, 0)),
                  pl.BlockSpec((SUBLANES, CONV_DIM), lambda i, j: (0, 0)),
                  pl.BlockSpec((SUBLANES, LANES), lambda i, j: (0, 0)),
                  pl.BlockSpec((1, GDN_DV), lambda i, j: (0, 0))],
        out_specs=[pl.BlockSpec((None, c, GDN_HEADS * GDN_DV), lambda i, j: (i, j, 0)),
                   pl.BlockSpec((None, GDN_HEADS, GDN_DK, GDN_DV), lambda i, j: (i, 0, 0, 0))],
        scratch_shapes=[pltpu.VMEM((c + SUBLANES, CONV_DIM), F32)],
        compiler_params=_cparams("parallel", "arbitrary"),
        name="gated_delta",
    )(qkv, gate, ab, cbuf, s0, cw, par, onorm_g.reshape(1, GDN_DV))
    return o, s


def _out_res_kernel(o_ref, w_ref, x_ref, m_ref, g2_ref, y_ref, h2_ref):
    def mod(i):
        return m_ref[:, i * D_MODEL:(i + 1) * D_MODEL]

    y = x_ref[...] + mod(2) * _dot(o_ref[...], w_ref[...])
    y_ref[...] = y
    h2_ref[...] = (_rms(y) * g2_ref[...] * (1.0 + mod(4)) + mod(3)).astype(h2_ref.dtype)


def _out_residual(o, w, x, mods, norm2_g, *, block_cap=512):
    b, t, d = x.shape
    kdim = o.shape[-1]
    tm = _row_block(t, block_cap)
    nb = t // tm
    n = b * t
    y, h2 = pl.pallas_call(
        _out_res_kernel,
        out_shape=[jax.ShapeDtypeStruct((n, d), F32), jax.ShapeDtypeStruct((n, d), BF16)],
        grid=(n // tm,),
        in_specs=[pl.BlockSpec((tm, kdim), lambda i: (i, 0)),
                  pl.BlockSpec((kdim, d), lambda i: (0, 0)),
                  pl.BlockSpec((tm, d), lambda i: (i, 0)),
                  pl.BlockSpec((None, 1, 6 * d), lambda i: (i // nb, 0, 0)),
                  pl.BlockSpec((1, d), lambda i: (0, 0))],
        out_specs=[pl.BlockSpec((tm, d), lambda i: (i, 0)), pl.BlockSpec((tm, d), lambda i: (i, 0))],
        compiler_params=_cparams("parallel"),
        name="out_proj_residual",
    )(o.reshape(n, kdim), w, x.reshape(n, d), mods, norm2_g.reshape(1, d))
    return y.reshape(b, t, d), h2


_CAND = [(j1, j2) for j1 in range(PEER_TOPK) for j2 in range(PEER_TOPK) if (j1 + 1) * (j2 + 1) <= PEER_TOPK]
_CAND_PAD = -(-len(_CAND) // SUBLANES) * SUBLANES


def _top_values(s, n_top):
    vals = [jnp.max(s, axis=0, keepdims=True)]
    for _ in range(1, n_top):
        vals.append(jnp.max(jnp.where(s < vals[-1], s, -jnp.inf), axis=0, keepdims=True))
    return vals


def _rank_among(s, vals):
    rank = jnp.zeros(s.shape, F32)
    for v in vals:
        rank = rank + jnp.where(s < v, 1.0, 0.0)
    return rank


def _peer_route_kernel(h2_ref, wq_ref, keys_ref, rank2_ref, e2_ref, n1_ref, e1_ref, q_scr):
    q_scr[...] = _dot(h2_ref[...], wq_ref[...]).astype(BF16)
    tm = h2_ref.shape[0]

    def head(h, carry):
        off = pl.multiple_of(h * 2 * PK_HALF, 2 * PK_HALF)
        s1 = _dot_nt(keys_ref[h, 0], q_scr[:, pl.ds(off, PK_HALF)])
        s2 = _dot_nt(keys_ref[h, 1], q_scr[:, pl.ds(off + PK_HALF, PK_HALF)])
        v1 = _top_values(s1, PEER_TOPK)
        v2 = _top_values(s2, PEER_TOPK)
        rank1 = _rank_among(s1, v1)
        rank2 = _rank_among(s2, v2)
        rows = [v1[j1] + v2[j2] for j1, j2 in _CAND]
        rows += [jnp.full((1, tm), -jnp.inf, F32)] * (_CAND_PAD - len(_CAND))
        cand = jnp.concatenate(rows, axis=0)
        tau = _top_values(cand, PEER_TOPK)[-1]
        sel = jnp.where(cand >= tau, 1.0, 0.0)
        z = jnp.sum(sel * jnp.exp(cand - cand[0:1, :]), axis=0, keepdims=True)
        n1 = jnp.zeros(s1.shape, F32)
        off_c = 0
        for j1 in range(PEER_TOPK):
            cnt = PEER_TOPK // (j1 + 1)
            n_j = jnp.sum(sel[off_c:off_c + cnt, :], axis=0, keepdims=True)
            n1 = jnp.where(rank1 == float(j1), n_j, n1)
            off_c += cnt
        rank2_ref[h] = rank2.astype(rank2_ref.dtype)
        e2_ref[h] = jnp.exp(s2 - v2[0]).astype(e2_ref.dtype)
        n1_ref[h] = n1
        e1_ref[h] = jnp.exp(s1 - v1[0]) / z
        return carry

    lax.fori_loop(0, PEER_HEADS, head, 0)


def _peer_route(h2, w_q, sub_keys, *, tm):
    n, d = h2.shape
    shp = (PEER_HEADS, N_KEYS, n)
    blk = pl.BlockSpec((PEER_HEADS, N_KEYS, tm), lambda i: (0, 0, i))
    return pl.pallas_call(
        _peer_route_kernel,
        out_shape=[jax.ShapeDtypeStruct(shp, BF16), jax.ShapeDtypeStruct(shp, BF16),
                   jax.ShapeDtypeStruct(shp, F32), jax.ShapeDtypeStruct(shp, F32)],
        grid=(n // tm,),
        in_specs=[pl.BlockSpec((tm, d), lambda i: (i, 0)),
                  pl.BlockSpec(w_q.shape, lambda i: (0, 0)),
                  pl.BlockSpec(sub_keys.shape, lambda i: (0, 0, 0, 0))],
        out_specs=[blk, blk, blk, blk],
        scratch_shapes=[pltpu.VMEM((tm, w_q.shape[1]), BF16)],
        compiler_params=_cparams("parallel"),
        name="peer_route",
    )(h2, w_q, sub_keys)


def _peer_dense_kernel(h2_ref, u_ref, vt_prev_ref, vt_cur_ref, rank2_ref, e2_ref, n1_ref, e1_ref, x_ref, m_ref,
                       fg_ref, y_ref, acc_a_ref, acc_b_ref, wa_a_ref, wa_b_ref, nb_ref, eb_ref, *, eb, final_norm):
    j = pl.program_id(1)
    n_pairs = pl.num_programs(1) - 1
    rows_per_half = eb // N_KEYS
    tm = h2_ref.shape[0]
    zero = jnp.zeros((), BF16)

    @pl.when(j == 0)
    def _():
        acc_a_ref[...] = jnp.zeros_like(acc_a_ref)
        acc_b_ref[...] = jnp.zeros_like(acc_b_ref)
        wa_b_ref[...] = jnp.zeros_like(wa_b_ref)

    def gated_activations(half, wa_ref):
        row0 = (j * 2 + half) * rows_per_half
        for h in range(PEER_HEADS):
            for r in range(rows_per_half):
                k = (half * PEER_HEADS + h) * rows_per_half + r
                n_row = n1_ref[h, pl.ds(row0 + r, 1), :].astype(BF16)
                e_row = e1_ref[h, pl.ds(row0 + r, 1), :].astype(BF16)
                nb_ref[k] = jnp.broadcast_to(n_row, (BF16_ROWS, tm))
                eb_ref[k] = jnp.broadcast_to(e_row, (BF16_ROWS, tm))
        pre = _dot_nt(u_ref[half * eb:(half + 1) * eb, :], h2_ref[...]).astype(BF16)
        for s0 in range(0, N_KEYS, BF16_ROWS):
            w = [None] * rows_per_half
            for h in range(PEER_HEADS):
                r2 = rank2_ref[h, s0:s0 + BF16_ROWS, :]
                e2 = e2_ref[h, s0:s0 + BF16_ROWS, :]
                for r in range(rows_per_half):
                    k = (half * PEER_HEADS + h) * rows_per_half + r
                    term = jnp.where(r2 < nb_ref[k], e2, zero) * eb_ref[k]
                    w[r] = term if w[r] is None else w[r] + term
            for r in range(rows_per_half):
                rows = slice(r * N_KEYS + s0, r * N_KEYS + s0 + BF16_ROWS)
                wa_ref[rows, :] = jax.nn.gelu(pre[rows, :]) * w[r]

    @pl.when(j < n_pairs)
    def _():
        acc_b_ref[...] += _dot(vt_prev_ref[...], wa_b_ref[...])
        gated_activations(0, wa_a_ref)
        gated_activations(1, wa_b_ref)
        acc_a_ref[...] += _dot(vt_cur_ref[...], wa_a_ref[...])

    @pl.when(j == n_pairs)
    def _():
        nseq, _, _ = m_ref.shape
        d = x_ref.shape[1]
        g2 = m_ref[:, :, 5 * D_MODEL:6 * D_MODEL]
        acc = acc_a_ref[...] + acc_b_ref[...] + _dot(vt_prev_ref[...], wa_b_ref[...])
        out = acc.T.reshape(nseq, tm // nseq, d)
        y = (x_ref[...].reshape(nseq, tm // nseq, d) + g2 * out).reshape(tm, d)
        if final_norm:
            y = _rms(y) * fg_ref[...]
        y_ref[...] = y


def _peer_dense(h2, route, u_b, vt_b, x, mods, final_g, *, tm, eb, final_norm):
    b, t, d = x.shape
    n = b * t
    nseq = max(1, tm // t)
    rank2, e2, n1, e1 = route
    tok = pl.BlockSpec((PEER_HEADS, N_KEYS, tm), lambda i, j: (0, 0, i))
    n_pairs = N_EXPERTS // (2 * eb)
    n_bcast = 2 * PEER_HEADS * (eb // N_KEYS)
    y = pl.pallas_call(
        functools.partial(_peer_dense_kernel, eb=eb, final_norm=final_norm),
        out_shape=jax.ShapeDtypeStruct((n, d), F32),
        grid=(n // tm, n_pairs + 1),
        in_specs=[pl.BlockSpec((tm, d), lambda i, j: (i, 0)),
                  pl.BlockSpec((2 * eb, d), lambda i, j: (jnp.minimum(j, n_pairs - 1), 0)),
                  pl.BlockSpec((d, eb), lambda i, j: (0, jnp.maximum(2 * j - 1, 0))),
                  pl.BlockSpec((d, eb), lambda i, j: (0, jnp.minimum(2 * j, 2 * n_pairs - 2))),
                  tok, tok, tok, tok,
                  pl.BlockSpec((tm, d), lambda i, j: (i, 0)),
                  pl.BlockSpec((nseq, 1, 6 * d), lambda i, j: ((i * tm) // t // nseq, 0, 0)),
                  pl.BlockSpec((1, d), lambda i, j: (0, 0))],
        out_specs=pl.BlockSpec((tm, d), lambda i, j: (i, 0)),
        scratch_shapes=[pltpu.VMEM((d, tm), F32), pltpu.VMEM((d, tm), F32),
                        pltpu.VMEM((eb, tm), BF16), pltpu.VMEM((eb, tm), BF16),
                        pltpu.VMEM((n_bcast, BF16_ROWS, tm), BF16), pltpu.VMEM((n_bcast, BF16_ROWS, tm), BF16)],
        compiler_params=_cparams("parallel", "arbitrary"),
        name="peer_experts",
    )(h2, u_b, vt_b, vt_b, rank2, e2, n1, e1, x.reshape(n, d), mods, final_g.reshape(1, d))
    return y.reshape(b, t, d)


def _peer(h2, x, mods, w_q, sub_keys, u_b, vt_b, final_g, *, final_norm):
    n = h2.shape[0]
    tm = _row_block(n, 512)
    route = _peer_route(h2, w_q, sub_keys, tm=min(tm, 256))
    return _peer_dense(h2, route, u_b, vt_b, x, mods, final_g, tm=tm, eb=512, final_norm=final_norm)


def _latent_kernel(x_ref, g_ref, wc_ref, wr_ref, cs_ref, kvg_ref, c_ref, kr_ref):
    xn = (_rms(x_ref[...]) * g_ref[...]).astype(BF16)
    c_ref[...] = _rms(_dot(xn, wc_ref[...])) * kvg_ref[...]
    p = _dot(xn, wr_ref[...]) * cs_ref[...]
    kr_ref[...] = p[:, :QK_ROPE] + p[:, QK_ROPE:]


def _shared_latent(x, kv_src_g, w_c, w_r2, cs_k, kv_norm_g, *, block_cap=512):
    b, t, d = x.shape
    tm = _row_block(t, block_cap)
    nb = t // tm
    n = b * t
    c_kv, k_r = pl.pallas_call(
        _latent_kernel,
        out_shape=[jax.ShapeDtypeStruct((n, KV_LORA), F32), jax.ShapeDtypeStruct((n, QK_ROPE), F32)],
        grid=(n // tm,),
        in_specs=[pl.BlockSpec((tm, d), lambda i: (i, 0)),
                  pl.BlockSpec((1, d), lambda i: (0, 0)),
                  pl.BlockSpec(w_c.shape, lambda i: (0, 0)),
                  pl.BlockSpec(w_r2.shape, lambda i: (0, 0)),
                  pl.BlockSpec((tm, 2 * QK_ROPE), lambda i: (i % nb, 0)),
                  pl.BlockSpec((1, KV_LORA), lambda i: (0, 0))],
        out_specs=[pl.BlockSpec((tm, KV_LORA), lambda i: (i, 0)), pl.BlockSpec((tm, QK_ROPE), lambda i: (i, 0))],
        compiler_params=_cparams("parallel"),
        name="mla_shared_latent",
    )(x.reshape(n, d), kv_src_g.reshape(1, d), w_c, w_r2, cs_k, kv_norm_g.reshape(1, KV_LORA))
    return c_kv.reshape(b, t, KV_LORA), k_r.reshape(b, t, QK_ROPE)


def _query_kernel(x_ref, m_ref, g_ref, wdq_ref, qg_ref, wq_ref, wqs_ref, c_ref, s_ref, q_ref):
    y = _rms(x_ref[...]) * g_ref[...]
    y = y * (1.0 + m_ref[:, D_MODEL:2 * D_MODEL]) + m_ref[:, 0:D_MODEL]
    cq = (_rms(_dot(y.astype(BF16), wdq_ref[...])) * qg_ref[...]).astype(BF16)
    q = _dot(cq, wq_ref[...])
    qs = _dot(cq, wqs_ref[...])
    cos = c_ref[...]
    sin = s_ref[...]
    for h in range(MLA_HEADS):
        sl = slice(h * QK_PAD, (h + 1) * QK_PAD)
        q_ref[:, sl] = (q[:, sl] * cos + qs[:, sl] * sin).astype(q_ref.dtype)


def _mla_queries(x, mods, norm1_g, w_dq, q_norm_g, w_q, w_qs, cos_q, sin_q, *, block_cap=512):
    b, t, d = x.shape
    tm = _row_block(t, block_cap)
    nb = t // tm
    n = b * t
    q = pl.pallas_call(
        _query_kernel,
        out_shape=jax.ShapeDtypeStruct((n, MLA_HEADS * QK_PAD), BF16),
        grid=(n // tm,),
        in_specs=[pl.BlockSpec((tm, d), lambda i: (i, 0)),
                  pl.BlockSpec((None, 1, 6 * d), lambda i: (i // nb, 0, 0)),
                  pl.BlockSpec((1, d), lambda i: (0, 0)),
                  pl.BlockSpec(w_dq.shape, lambda i: (0, 0)),
                  pl.BlockSpec((1, Q_LORA), lambda i: (0, 0)),
                  pl.BlockSpec(w_q.shape, lambda i: (0, 0)),
                  pl.BlockSpec(w_qs.shape, lambda i: (0, 0)),
                  pl.BlockSpec((tm, QK_PAD), lambda i: (i % nb, 0)),
                  pl.BlockSpec((tm, QK_PAD), lambda i: (i % nb, 0))],
        out_specs=pl.BlockSpec((tm, MLA_HEADS * QK_PAD), lambda i: (i, 0)),
        compiler_params=_cparams("parallel"),
        name="mla_queries",
    )(x.reshape(n, d), mods, norm1_g.reshape(1, d), w_dq, q_norm_g.reshape(1, Q_LORA), w_q, w_qs, cos_q, sin_q)
    return q.reshape(b, t, MLA_HEADS * QK_PAD)


def _kv_kernel(c_ref, kr_ref, wuk_ref, wuv_ref, k_ref, v_ref):
    cb = c_ref[...].astype(BF16)
    kn = _dot(cb, wuk_ref[...])
    v_ref[...] = _dot(cb, wuv_ref[...]).astype(v_ref.dtype)
    tm = c_ref.shape[0]
    tail = jnp.concatenate([kr_ref[...], jnp.zeros((tm, QK_PAD - QK_NOPE - QK_ROPE), F32)], axis=1)
    tail = tail.astype(k_ref.dtype)
    for h in range(MLA_HEADS):
        k_ref[:, h * QK_PAD:h * QK_PAD + QK_NOPE] = kn[:, h * QK_NOPE:(h + 1) * QK_NOPE].astype(k_ref.dtype)
        k_ref[:, h * QK_PAD + QK_NOPE:(h + 1) * QK_PAD] = tail


def _mla_keys_values(c_all, kr_all, w_uk, w_uv, *, block_cap=512):
    b, tk, _ = c_all.shape
    tm = _row_block(tk, block_cap)
    n = b * tk
    k, v = pl.pallas_call(
        _kv_kernel,
        out_shape=[jax.ShapeDtypeStruct((n, MLA_HEADS * QK_PAD), BF16),
                   jax.ShapeDtypeStruct((n, MLA_HEADS * V_HEAD), BF16)],
        grid=(n // tm,),
        in_specs=[pl.BlockSpec((tm, KV_LORA), lambda i: (i, 0)),
                  pl.BlockSpec((tm, QK_ROPE), lambda i: (i, 0)),
                  pl.BlockSpec(w_uk.shape, lambda i: (0, 0)),
                  pl.BlockSpec(w_uv.shape, lambda i: (0, 0))],
        out_specs=[pl.BlockSpec((tm, MLA_HEADS * QK_PAD), lambda i: (i, 0)),
                   pl.BlockSpec((tm, MLA_HEADS * V_HEAD), lambda i: (i, 0))],
        compiler_params=_cparams("parallel"),
        name="mla_keys_values",
    )(c_all.reshape(n, KV_LORA), kr_all.reshape(n, QK_ROPE), w_uk, w_uv)
    return k.reshape(b, tk, -1), v.reshape(b, tk, -1)


def _attn_kernel(q_ref, k_ref, v_ref, o_ref, *, tk, pos0, n_keys):
    tq = q_ref.shape[0]
    q = q_ref[...]
    q_pos0 = pos0 + pl.program_id(2) * tq
    q_chunk = (q_pos0 + lax.broadcasted_iota(jnp.int32, (tq, 1), 0)) // CHUNK
    k_end = jnp.minimum(((q_pos0 + tq - 1) // CHUNK + 1) * CHUNK, n_keys)
    n_blocks = (k_end + tk - 1) // tk
    n_full = jnp.minimum((q_pos0 // CHUNK + 1) * CHUNK, n_keys) // tk
    c_exp = SM_SCALE * math.log2(math.e)

    def block(kb, carry, masked):
        m, l, acc = carry
        start = pl.multiple_of(kb * tk, tk)
        s = _dot_nt(q, k_ref[pl.ds(start, tk), :])
        if masked:
            k_pos = start + lax.broadcasted_iota(jnp.int32, (1, tk), 1)
            visible = (k_pos // CHUNK <= q_chunk) & (k_pos < n_keys)
            s = jnp.where(visible, s, -jnp.inf)
        m_new = jnp.maximum(m, jnp.max(s, axis=-1, keepdims=True))
        p = jnp.exp2((s - m_new) * c_exp)
        alpha = jnp.exp2((m - m_new) * c_exp)
        l = alpha * l + jnp.sum(p, axis=-1, keepdims=True)
        acc = alpha * acc + _dot(p.astype(BF16), v_ref[pl.ds(start, tk), :])
        return m_new, l, acc

    init = (jnp.full((tq, 1), -jnp.inf, F32), jnp.zeros((tq, 1), F32), jnp.zeros((tq, V_HEAD), F32))
    carry = lax.fori_loop(0, n_full, functools.partial(block, masked=False), init)
    _, l, acc = lax.fori_loop(n_full, n_blocks, functools.partial(block, masked=True), carry)
    o_ref[...] = (acc / l).astype(o_ref.dtype)


def _attention(q, k, v, *, pos0, n_keys):
    b, t, _ = q.shape
    tkeys = k.shape[1]
    tq = _row_block(t, 256)
    tk = _row_block(tkeys, 256)
    return pl.pallas_call(
        functools.partial(_attn_kernel, tk=tk, pos0=pos0, n_keys=n_keys),
        out_shape=jax.ShapeDtypeStruct((b, t, MLA_HEADS * V_HEAD), BF16),
        grid=(b, MLA_HEADS, t // tq),
        in_specs=[pl.BlockSpec((None, tq, QK_PAD), lambda bi, h, i: (bi, i, h)),
                  pl.BlockSpec((None, tkeys, QK_PAD), lambda bi, h, i: (bi, 0, h)),
                  pl.BlockSpec((None, tkeys, V_HEAD), lambda bi, h, i: (bi, 0, h))],
        out_specs=pl.BlockSpec((None, tq, V_HEAD), lambda bi, h, i: (bi, i, h)),
        compiler_params=_cparams("parallel", "parallel", "arbitrary"),
        name="mla_attention",
    )(q, k, v)


def _rope_tables(pos):
    half = QK_ROPE // 2
    inv = ROPE_THETA ** (-jnp.arange(half, dtype=F32) / half)
    ang = pos.astype(F32)[:, None] * inv[None, :]
    cos, sin = jnp.cos(ang), jnp.sin(ang)
    t = pos.shape[0]
    cs_k = jnp.concatenate([cos, cos, -sin, sin], axis=1)
    one = jnp.ones((t, QK_NOPE), F32)
    zero_n = jnp.zeros((t, QK_NOPE), F32)
    zero_p = jnp.zeros((t, QK_PAD - QK_NOPE - QK_ROPE), F32)
    cos_q = jnp.concatenate([one, cos, cos, zero_p], axis=1)
    sin_q = jnp.concatenate([zero_n, -sin, sin, zero_p], axis=1)
    return cs_k, cos_q, sin_q


def _swap_halves(w):
    half = QK_ROPE // 2
    return jnp.concatenate([w[..., half:], w[..., :half]], axis=-1)


def _prep_weights(p):
    o1 = CONV_DIM + GDN_HEADS * GDN_DV
    w = {}
    w_in = p['gdn_w_in']
    w['gdn_qkv'] = w_in[:, :, :CONV_DIM].astype(BF16)
    w['gdn_gate'] = w_in[:, :, CONV_DIM:o1].astype(BF16)
    w['gdn_ab'] = jnp.pad(w_in[:, :, o1:], ((0, 0), (0, 0), (0, LANES - 2 * GDN_HEADS))).astype(BF16)
    w['gdn_o'] = p['gdn_w_o'].astype(BF16)
    w_dkv = p['mla_w_dkv']
    w['dkv_c'] = w_dkv[:, :KV_LORA].astype(BF16)
    rope_cols = w_dkv[:, KV_LORA:]
    w['dkv_r2'] = jnp.concatenate([rope_cols, _swap_halves(rope_cols)], axis=1).astype(BF16)
    w['uk'] = p['mla_w_uk'].astype(BF16)
    w['uv'] = p['mla_w_uv'].astype(BF16)
    w['dq'] = p['mla_w_dq'].astype(BF16)
    n_b = p['mla_w_uq'].shape[0]
    uq = p['mla_w_uq'].reshape(n_b, Q_LORA, MLA_HEADS, QK_NOPE + QK_ROPE)
    pad = jnp.zeros((n_b, Q_LORA, MLA_HEADS, QK_PAD - QK_NOPE - QK_ROPE), F32)
    w['uq'] = jnp.concatenate([uq, pad], axis=-1).reshape(n_b, Q_LORA, MLA_HEADS * QK_PAD).astype(BF16)
    uq_s = jnp.concatenate([jnp.zeros_like(uq[..., :QK_NOPE]), _swap_halves(uq[..., QK_NOPE:]), pad], axis=-1)
    w['uq_swap'] = uq_s.reshape(n_b, Q_LORA, MLA_HEADS * QK_PAD).astype(BF16)
    w['mla_o'] = p['mla_w_o'].astype(BF16)
    w['peer_q'] = p['peer_w_q'].astype(BF16)
    w['peer_keys'] = p['peer_sub_keys'].astype(BF16)
    w['peer_u'] = p['peer_u'].astype(BF16)
    w['peer_vt'] = jnp.swapaxes(p['peer_v'], 1, 2).astype(BF16)
    return w


def _trunk(x, mods, pos0, conv_bufs, delta_states, past_c, past_kr, p, w):
    b, t, d = x.shape
    depth = mods.shape[0]
    n_a = p['gdn_w_in'].shape[0]
    new_states, new_bufs = [], []
    c_kv = k_r = k_cat = v_all = None
    n_keys = past_c.shape[1] + t
    for l in range(depth):
        m = mods[l]
        if l < n_a:
            qkv, gate, ab = _norm_matmul(x, m, p['norm1_g'][l], [w['gdn_qkv'][l], w['gdn_gate'][l], w['gdn_ab'][l]],
                                         [F32, F32, F32], mod_off=0)
            o, s = _gdn(qkv, gate, ab, conv_bufs[l], delta_states[l], p['gdn_conv_w'][l], p['gdn_a_log'][l],
                        p['gdn_dt_bias'][l], p['gdn_onorm_g'][l])
            new_bufs.append(qkv[:, t - (CONV_W - 1):, :])
            new_states.append(s)
            w_o = w['gdn_o'][l]
        else:
            j = l - n_a
            cs_k, cos_q, sin_q = _rope_tables(pos0 + jnp.arange(t))
            if j == 0:
                c_kv, k_r = _shared_latent(x, p['kv_src_g'], w['dkv_c'], w['dkv_r2'], cs_k, p['mla_kv_norm_g'])
                c_all = jnp.concatenate([past_c, c_kv], axis=1)
                kr_all = jnp.concatenate([past_kr, k_r], axis=1)
                pad = -n_keys % LANES
                c_all = jnp.pad(c_all, ((0, 0), (0, pad), (0, 0)))
                kr_all = jnp.pad(kr_all, ((0, 0), (0, pad), (0, 0)))
                k_cat, v_all = _mla_keys_values(c_all, kr_all, w['uk'], w['uv'])
            q = _mla_queries(x, m, p['norm1_g'][l], w['dq'][j], p['mla_q_norm_g'][j], w['uq'][j], w['uq_swap'][j],
                             cos_q, sin_q)
            o = _attention(q, k_cat, v_all, pos0=pos0, n_keys=n_keys)
            w_o = w['mla_o'][j]
        x, h2 = _out_residual(o, w_o, x, m, p['norm2_g'][l])
        x = _peer(h2, x, m, w['peer_q'][l], w['peer_keys'][l], w['peer_u'][l], w['peer_vt'][l], p['final_g'],
                  final_norm=(l == depth - 1))
    return x, jnp.stack(new_states), jnp.stack(new_bufs), c_kv, k_r


def kernel(x_prompt, x_sample, c_prompt, c_sample, state_delta, state_conv, cache_kv_latent, cache_k_rope, w_mod, b_mod, norm1_g, norm2_g, final_g, gdn_w_in, gdn_conv_w, gdn_a_log, gdn_dt_bias, gdn_onorm_g, gdn_w_o, kv_src_g, mla_w_dkv, mla_kv_norm_g, mla_w_uk, mla_w_uv, mla_w_dq, mla_q_norm_g, mla_w_uq, mla_w_o, peer_w_q, peer_sub_keys, peer_u, peer_v):
    p = dict(norm1_g=norm1_g, norm2_g=norm2_g, final_g=final_g, gdn_w_in=gdn_w_in, gdn_conv_w=gdn_conv_w,
             gdn_a_log=gdn_a_log, gdn_dt_bias=gdn_dt_bias, gdn_onorm_g=gdn_onorm_g, gdn_w_o=gdn_w_o,
             kv_src_g=kv_src_g, mla_w_dkv=mla_w_dkv, mla_kv_norm_g=mla_kv_norm_g, mla_w_uk=mla_w_uk,
             mla_w_uv=mla_w_uv, mla_w_dq=mla_w_dq, mla_q_norm_g=mla_q_norm_g, mla_w_uq=mla_w_uq, mla_w_o=mla_w_o,
             peer_w_q=peer_w_q, peer_sub_keys=peer_sub_keys, peer_u=peer_u, peer_v=peer_v)
    w = _prep_weights(p)
    bp, tp, d = x_prompt.shape
    bs, ts, _ = x_sample.shape
    n_a = gdn_w_in.shape[0]

    rows = -(-(bp + bs) // SUBLANES) * SUBLANES
    c_all = jnp.pad(jnp.concatenate([c_prompt, c_sample], axis=0), ((0, rows - bp - bs), (0, 0)))
    mods = _modulation(c_all, w_mod, b_mod)
    mods_p = mods[:, :bp, None, :]
    mods_s = mods[:, bp:bp + bs, None, :]

    zero_bufs = jnp.zeros((n_a, bp, CONV_W - 1, CONV_DIM), F32)
    zero_states = jnp.zeros((n_a, bp, GDN_HEADS, GDN_DK, GDN_DV), F32)
    no_c = jnp.zeros((bp, 0, KV_LORA), F32)
    no_kr = jnp.zeros((bp, 0, QK_ROPE), F32)
    y_p, p_delta, p_conv, p_lat, p_kr = _trunk(x_prompt, mods_p, 0, zero_bufs, zero_states, no_c, no_kr, p, w)
    past = cache_kv_latent.shape[1]
    y_s, s_delta, s_conv, s_lat, s_kr = _trunk(x_sample, mods_s, past, state_conv, state_delta, cache_kv_latent,
                                               cache_k_rope, p, w)
    return (y_p, y_s, p_delta, p_conv, p_lat, p_kr, s_delta, s_conv, s_lat, s_kr)
```

```python
import functools
import math

import jax
import jax.numpy as jnp
from jax import lax
from jax.experimental import pallas as pl
from jax.experimental.pallas import tpu as pltpu

F32 = jnp.float32
BF16 = jnp.bfloat16
HIGHEST = lax.Precision.HIGHEST

D_MODEL = 1024
EPS = 1e-6
CHUNK = 64

GDN_HEADS = 8
GDN_DK = 128
GDN_DV = 128
CONV_W = 4
GDN_QK = GDN_HEADS * GDN_DK
CONV_DIM = GDN_HEADS * (2 * GDN_DK + GDN_DV)

MLA_HEADS = 8
QK_NOPE = 128
QK_ROPE = 64
V_HEAD = 128
KV_LORA = 256
Q_LORA = 384
ROPE_THETA = 10000.0
SM_SCALE = (QK_NOPE + QK_ROPE) ** -0.5
QK_PAD = 256
V_PAD = 256
ATTN_KEY_BLOCK = 1024

N_KEYS = 128
N_EXPERTS = N_KEYS * N_KEYS
PEER_HEADS = 8
PEER_TOPK = 16
PK_HALF = 128
PEER_EXPERT_BLOCK = 512

LANES = 128
SUBLANES = 8
BF16_ROWS = 16
VMEM_LIMIT = 56 * 1024 * 1024


def _cparams(*sem):
    return pltpu.CompilerParams(dimension_semantics=sem, vmem_limit_bytes=VMEM_LIMIT)


def _silu(x):
    return x * jax.nn.sigmoid(x)


def _softplus(x):
    return jnp.maximum(x, 0.0) + jnp.log1p(jnp.exp(-jnp.abs(x)))


def _rms(x):
    return x * lax.rsqrt(jnp.mean(x * x, axis=-1, keepdims=True) + EPS)


def _dot(a, b, **kw):
    return jnp.dot(a, b, preferred_element_type=F32, **kw)


def _dot_nt(a, b, **kw):
    return lax.dot_general(a, b, (((1,), (1,)), ((), ())), preferred_element_type=F32, **kw)


def _dot_tn(a, b, **kw):
    return lax.dot_general(a, b, (((0,), (0,)), ((), ())), preferred_element_type=F32, **kw)


def _split_bf16(a):
    hi = a.astype(BF16)
    return hi, (a - hi.astype(F32)).astype(BF16)


def _dot3(a, b):
    ah, al = _split_bf16(a)
    bh, bl = _split_bf16(b)
    return _dot(jnp.concatenate([ah, ah, al], axis=1), jnp.concatenate([bh, bl, bh], axis=0))


def _row_block(t, cap):
    b = min(t, cap)
    while t % b:
        b //= 2
    return b


def _mod_kernel(c_ref, w_ref, b_ref, o_ref):
    a = _silu(c_ref[...]).astype(BF16)
    o_ref[...] = _dot(a, w_ref[...].astype(BF16)) + b_ref[...]


def _modulation(c_pad, w_mod, b_mod):
    depth, d, n = w_mod.shape
    r = c_pad.shape[0]
    tn = 1024
    return pl.pallas_call(
        _mod_kernel,
        out_shape=jax.ShapeDtypeStruct((depth, r, n), F32),
        grid=(depth, n // tn),
        in_specs=[pl.BlockSpec((r, d), lambda l, j: (0, 0)),
                  pl.BlockSpec((None, d, tn), lambda l, j: (l, 0, j)),
                  pl.BlockSpec((None, 1, tn), lambda l, j: (l, 0, j))],
        out_specs=pl.BlockSpec((None, r, tn), lambda l, j: (l, 0, j)),
        compiler_params=_cparams("arbitrary", "arbitrary"),
        name="adaln_modulation",
    )(c_pad, w_mod, b_mod.reshape(depth, 1, n))


def _norm_mm_kernel(*refs, n_w, mod_off):
    x_ref, m_ref, g_ref = refs[:3]
    w_refs = refs[3:3 + n_w]
    o_refs = refs[3 + n_w:]
    y = _rms(x_ref[...]) * g_ref[...]
    if mod_off is not None:
        sh = m_ref[:, mod_off * D_MODEL:(mod_off + 1) * D_MODEL]
        sc = m_ref[:, (mod_off + 1) * D_MODEL:(mod_off + 2) * D_MODEL]
        y = y * (1.0 + sc) + sh
    yb = y.astype(BF16)
    for w_ref, o_ref in zip(w_refs, o_refs):
        o_ref[...] = _dot(yb, w_ref[...]).astype(o_ref.dtype)


def _norm_matmul(x, mods, g, weights, out_dtypes, *, mod_off, block_cap=512):
    b, t, d = x.shape
    tm = _row_block(t, block_cap)
    nb = t // tm
    n = b * t
    x2 = x.reshape(n, d)
    in_specs = [pl.BlockSpec((tm, d), lambda i: (i, 0)),
                pl.BlockSpec((None, 1, 6 * d), lambda i: (i // nb, 0, 0)),
                pl.BlockSpec((1, d), lambda i: (0, 0))]
    in_specs += [pl.BlockSpec(w.shape, lambda i: (0, 0)) for w in weights]
    outs = pl.pallas_call(
        functools.partial(_norm_mm_kernel, n_w=len(weights), mod_off=mod_off),
        out_shape=[jax.ShapeDtypeStruct((n, w.shape[1]), dt) for w, dt in zip(weights, out_dtypes)],
        grid=(n // tm,),
        in_specs=in_specs,
        out_specs=[pl.BlockSpec((tm, w.shape[1]), lambda i: (i, 0)) for w in weights],
        compiler_params=_cparams("parallel"),
        name="norm_matmul",
    )(x2, mods, g.reshape(1, d), *weights)
    return [o.reshape(b, t, o.shape[-1]) for o in outs]


def _gdn_kernel(qkv_ref, gate_ref, ab_ref, cbuf_ref, s0_ref, cw_ref, par_ref, og_ref,
                o_ref, s_ref, xp_ref, *, chunk):
    c = chunk
    step = pl.program_id(1)

    @pl.when(step == 0)
    def _():
        s_ref[...] = s0_ref[...]
        xp_ref[0:SUBLANES, :] = cbuf_ref[...]

    xp_ref[SUBLANES:SUBLANES + c, :] = qkv_ref[...]
    base = SUBLANES - (CONV_W - 1)
    conv = xp_ref[base:base + c, :] * cw_ref[0:1, :]
    for w in range(1, CONV_W):
        conv = conv + xp_ref[base + w:base + w + c, :] * cw_ref[w:w + 1, :]
    xp_ref[0:SUBLANES, :] = xp_ref[c:c + SUBLANES, :]
    act = _silu(conv)

    ab = ab_ref[...]
    g_all = -jnp.exp(par_ref[0:1, :]) * _softplus(ab + par_ref[1:2, :])
    beta_all = jax.nn.sigmoid(ab)

    row = lax.broadcasted_iota(jnp.int32, (c, c), 0)
    col = lax.broadcasted_iota(jnp.int32, (c, c), 1)
    causal = row >= col
    strict = row > col
    eye = (row == col).astype(F32)
    lower = causal.astype(F32)
    upper = (row <= col).astype(F32)
    r128 = lax.broadcasted_iota(jnp.int32, (LANES, LANES), 0)
    c128 = lax.broadcasted_iota(jnp.int32, (LANES, LANES), 1)
    eye128 = (r128 == c128).astype(F32)

    d_col_all = _dot(lower, g_all, precision=HIGHEST)
    g_t = _dot_nt(eye128, g_all, precision=HIGHEST)
    d_row_all = _dot(g_t, upper, precision=HIGHEST)

    n_double = int(math.log2(c)) - 1
    keep_s = lax.broadcasted_iota(jnp.int32, (c, 2 * c), 1) < c
    heads = range(GDN_HEADS)
    q, k, v, beta, d_col, d_last, decay, e_col, kb, kq = ([None] * GDN_HEADS for _ in range(10))
    for h in heads:
        qh = act[:, h * GDN_DK:(h + 1) * GDN_DK]
        kh = act[:, GDN_QK + h * GDN_DK:GDN_QK + (h + 1) * GDN_DK]
        v[h] = act[:, 2 * GDN_QK + h * GDN_DV:2 * GDN_QK + (h + 1) * GDN_DV]
        q[h] = qh * lax.rsqrt(jnp.sum(qh * qh, axis=-1, keepdims=True) + EPS) * (GDN_DK ** -0.5)
        k[h] = kh * lax.rsqrt(jnp.sum(kh * kh, axis=-1, keepdims=True) + EPS)
        beta[h] = beta_all[:, GDN_HEADS + h:GDN_HEADS + h + 1]
        d_col[h] = d_col_all[:, h:h + 1]
        d_last[h] = d_col_all[c - 1:c, h:h + 1]
        decay[h] = jnp.exp(jnp.where(causal, d_col[h] - d_row_all[h:h + 1, :], -jnp.inf))
        e_col[h] = jnp.exp(d_col[h])
        kb[h] = k[h] * beta[h]
    for h in heads:
        kq[h] = _dot_nt(jnp.concatenate([kb[h], q[h]], axis=0).astype(BF16), k[h].astype(BF16))
    m = [-jnp.where(strict, kq[h][:c] * decay[h], 0.0) for h in heads]
    x = [jnp.concatenate([eye, m[h]], axis=1) for h in heads]
    for _ in range(n_double):
        for h in heads:
            x[h] = jnp.where(keep_s, x[h], 0.0) + _dot3(m[h], x[h])
            m[h] = x[h][:, c:]
    p = [x[h][:, :c] + _dot3(m[h], x[h][:, :c]) for h in heads]
    sol = [_dot3(p[h], jnp.concatenate([v[h] * beta[h], kb[h] * e_col[h]], axis=1)) for h in heads]
    s = [s_ref[h] for h in heads]
    ws = [_dot(jnp.concatenate([sol[h][:, GDN_DV:], q[h] * e_col[h]], axis=0).astype(BF16), s[h].astype(BF16))
          for h in heads]
    vnb = [(sol[h][:, :GDN_DV] - ws[h][:c]).astype(BF16) for h in heads]
    o = [ws[h][c:] + _dot((kq[h][c:] * decay[h]).astype(BF16), vnb[h]) for h in heads]
    for h in heads:
        k_dec = (k[h] * jnp.exp(d_last[h] - d_col[h])).astype(BF16)
        s_ref[h] = s[h] * jnp.exp(d_last[h]) + _dot_tn(k_dec, vnb[h])
    for h in heads:
        gate = gate_ref[:, h * GDN_DV:(h + 1) * GDN_DV]
        o_ref[:, h * GDN_DV:(h + 1) * GDN_DV] = (_rms(o[h]) * og_ref[...] * _silu(gate)).astype(o_ref.dtype)


def _gdn(qkv, gate, ab, conv_buf, s0, conv_w, a_log, dt_bias, onorm_g):
    b, t, _ = qkv.shape
    c = min(CHUNK, t)
    cbuf = jnp.pad(conv_buf, ((0, 0), (SUBLANES - (CONV_W - 1), 0), (0, 0)))
    cw = jnp.pad(conv_w, ((0, SUBLANES - CONV_W), (0, 0)))
    par = jnp.zeros((SUBLANES, LANES), F32)
    par = par.at[0, :GDN_HEADS].set(a_log).at[1, :GDN_HEADS].set(dt_bias)
    o, s = pl.pallas_call(
        functools.partial(_gdn_kernel, chunk=c),
        out_shape=[jax.ShapeDtypeStruct((b, t, GDN_HEADS * GDN_DV), BF16),
                   jax.ShapeDtypeStruct((b, GDN_HEADS, GDN_DK, GDN_DV), F32)],
        grid=(b, t // c),
        in_specs=[pl.BlockSpec((None, c, CONV_DIM), lambda i, j: (i, j, 0)),
                  pl.BlockSpec((None, c, GDN_HEADS * GDN_DV), lambda i, j: (i, j, 0)),
                  pl.BlockSpec((None, c, LANES), lambda i, j: (i, j, 0)),
                  pl.BlockSpec((None, SUBLANES, CONV_DIM), lambda i, j: (i, 0, 0)),
                  pl.BlockSpec((None, GDN_HEADS, GDN_DK, GDN_DV), lambda i, j: (i, 0, 0, 0)),
                  pl.BlockSpec((SUBLANES, CONV_DIM), lambda i, j: (0, 0)),
                  pl.BlockSpec((SUBLANES, LANES), lambda i, j: (0, 0)),
                  pl.BlockSpec((1, GDN_DV), lambda i, j: (0, 0))],
        out_specs=[pl.BlockSpec((None, c, GDN_HEADS * GDN_DV), lambda i, j: (i, j, 0)),
                   pl.BlockSpec((None, GDN_HEADS, GDN_DK, GDN_DV), lambda i, j: (i, 0, 0, 0))],
        scratch_shapes=[pltpu.VMEM((c + SUBLANES, CONV_DIM), F32)],
        compiler_params=_cparams("parallel", "arbitrary"),
        name="gated_delta",
    )(qkv, gate, ab, cbuf, s0, cw, par, onorm_g.reshape(1, GDN_DV))
    return o, s


def _out_res_kernel(o_ref, w_ref, x_ref, m_ref, g2_ref, y_ref, h2_ref):
    def mod(i):
        return m_ref[:, i * D_MODEL:(i + 1) * D_MODEL]

    y = x_ref[...] + mod(2) * _dot(o_ref[...], w_ref[...])
    y_ref[...] = y
    h2_ref[...] = (_rms(y) * g2_ref[...] * (1.0 + mod(4)) + mod(3)).astype(h2_ref.dtype)


def _out_residual(o, w, x, mods, norm2_g, *, block_cap=512):
    b, t, d = x.shape
    kdim = o.shape[-1]
    tm = _row_block(t, block_cap)
    nb = t // tm
    n = b * t
    y, h2 = pl.pallas_call(
        _out_res_kernel,
        out_shape=[jax.ShapeDtypeStruct((n, d), F32), jax.ShapeDtypeStruct((n, d), BF16)],
        grid=(n // tm,),
        in_specs=[pl.BlockSpec((tm, kdim), lambda i: (i, 0)),
                  pl.BlockSpec((kdim, d), lambda i: (0, 0)),
                  pl.BlockSpec((tm, d), lambda i: (i, 0)),
                  pl.BlockSpec((None, 1, 6 * d), lambda i: (i // nb, 0, 0)),
                  pl.BlockSpec((1, d), lambda i: (0, 0))],
        out_specs=[pl.BlockSpec((tm, d), lambda i: (i, 0)), pl.BlockSpec((tm, d), lambda i: (i, 0))],
        compiler_params=_cparams("parallel"),
        name="out_proj_residual",
    )(o.reshape(n, kdim), w, x.reshape(n, d), mods, norm2_g.reshape(1, d))
    return y.reshape(b, t, d), h2


_CAND = [(j1, j2) for j1 in range(PEER_TOPK) for j2 in range(PEER_TOPK) if (j1 + 1) * (j2 + 1) <= PEER_TOPK]
_CAND_PAD = -(-len(_CAND) // SUBLANES) * SUBLANES


def _top_values(s, n_top):
    vals = [jnp.max(s, axis=0, keepdims=True)]
    for _ in range(1, n_top):
        vals.append(jnp.max(jnp.where(s < vals[-1], s, -jnp.inf), axis=0, keepdims=True))
    return vals


def _rank_among(s, vals):
    rank = jnp.zeros(s.shape, F32)
    for v in vals:
        rank = rank + jnp.where(s < v, 1.0, 0.0)
    return rank


def _peer_route_kernel(h2_ref, wq_ref, keys_ref, rank2_ref, e2_ref, n1_ref, e1_ref, q_scr):
    q_scr[...] = _dot(h2_ref[...], wq_ref[...]).astype(BF16)
    tm = h2_ref.shape[0]

    def head(h, carry):
        off = pl.multiple_of(h * 2 * PK_HALF, 2 * PK_HALF)
        s1 = _dot_nt(keys_ref[h, 0], q_scr[:, pl.ds(off, PK_HALF)])
        s2 = _dot_nt(keys_ref[h, 1], q_scr[:, pl.ds(off + PK_HALF, PK_HALF)])
        v1 = _top_values(s1, PEER_TOPK)
        v2 = _top_values(s2, PEER_TOPK)
        rank1 = _rank_among(s1, v1)
        rank2 = _rank_among(s2, v2)
        rows = [v1[j1] + v2[j2] for j1, j2 in _CAND]
        rows += [jnp.full((1, tm), -jnp.inf, F32)] * (_CAND_PAD - len(_CAND))
        cand = jnp.concatenate(rows, axis=0)
        tau = _top_values(cand, PEER_TOPK)[-1]
        sel = jnp.where(cand >= tau, 1.0, 0.0)
        z = jnp.sum(sel * jnp.exp(cand - cand[0:1, :]), axis=0, keepdims=True)
        n1 = jnp.zeros(s1.shape, F32)
        off_c = 0
        for j1 in range(PEER_TOPK):
            cnt = PEER_TOPK // (j1 + 1)
            n_j = jnp.sum(sel[off_c:off_c + cnt, :], axis=0, keepdims=True)
            n1 = jnp.where(rank1 == float(j1), n_j, n1)
            off_c += cnt
        rank2_ref[h] = rank2.astype(rank2_ref.dtype)
        e2_ref[h] = jnp.exp(s2 - v2[0]).astype(e2_ref.dtype)
        n1_ref[h] = n1
        e1_ref[h] = jnp.exp(s1 - v1[0]) / z
        return carry

    lax.fori_loop(0, PEER_HEADS, head, 0)


def _peer_route(h2, w_q, sub_keys, *, tm):
    n, d = h2.shape
    shp = (PEER_HEADS, N_KEYS, n)
    blk = pl.BlockSpec((PEER_HEADS, N_KEYS, tm), lambda i: (0, 0, i))
    return pl.pallas_call(
        _peer_route_kernel,
        out_shape=[jax.ShapeDtypeStruct(shp, BF16), jax.ShapeDtypeStruct(shp, BF16),
                   jax.ShapeDtypeStruct(shp, F32), jax.ShapeDtypeStruct(shp, F32)],
        grid=(n // tm,),
        in_specs=[pl.BlockSpec((tm, d), lambda i: (i, 0)),
                  pl.BlockSpec(w_q.shape, lambda i: (0, 0)),
                  pl.BlockSpec(sub_keys.shape, lambda i: (0, 0, 0, 0))],
        out_specs=[blk, blk, blk, blk],
        scratch_shapes=[pltpu.VMEM((tm, w_q.shape[1]), BF16)],
        compiler_params=_cparams("parallel"),
        name="peer_route",
    )(h2, w_q, sub_keys)


def _peer_dense_kernel(h2_ref, u_ref, vt_prev_ref, vt_cur_ref, rank2_ref, e2_ref, n1_ref, e1_ref, x_ref, m_ref,
                       fg_ref, y_ref, acc_a_ref, acc_b_ref, wa_a_ref, wa_b_ref, gate_a_ref, gate_b_ref, nb_ref, eb_ref,
                       *, eb, final_norm):
    j = pl.program_id(1)
    n_pairs = pl.num_programs(1) - 1
    rows_per_half = eb // N_KEYS
    tm = h2_ref.shape[0]
    zero = jnp.zeros((), BF16)

    @pl.when(j == 0)
    def _():
        acc_a_ref[...] = jnp.zeros_like(acc_a_ref)
        acc_b_ref[...] = jnp.zeros_like(acc_b_ref)
        wa_b_ref[...] = jnp.zeros_like(wa_b_ref)

    tn = tm // 2
    s_tiles = list(range(0, N_KEYS, BF16_ROWS))

    def stage_rows(half):
        for h in range(PEER_HEADS):
            n_all = n1_ref[h].astype(BF16)
            e_all = e1_ref[h].astype(BF16)
            for r in range(rows_per_half):
                k = (half * PEER_HEADS + h) * rows_per_half + r
                row = half * rows_per_half + r
                nb_ref[k] = jnp.broadcast_to(n_all[row:row + 1, :], (BF16_ROWS, tm))
                eb_ref[k] = jnp.broadcast_to(e_all[row:row + 1, :], (BF16_ROWS, tm))

    def gate_tiles(half, gate_ref, tiles):
        for s0 in tiles:
            w = [None] * rows_per_half
            for h in range(PEER_HEADS):
                r2 = rank2_ref[h, s0:s0 + BF16_ROWS, :]
                e2 = e2_ref[h, s0:s0 + BF16_ROWS, :]
                for r in range(rows_per_half):
                    k = (half * PEER_HEADS + h) * rows_per_half + r
                    term = jnp.where(r2 < nb_ref[k], e2, zero) * eb_ref[k]
                    w[r] = term if w[r] is None else w[r] + term
            for r in range(rows_per_half):
                gate_ref[r * N_KEYS + s0:r * N_KEYS + s0 + BF16_ROWS, :] = w[r]

    def pre_act(half, piece):
        lanes = slice(piece * tn, (piece + 1) * tn)
        return _dot_nt(u_ref[half * eb:(half + 1) * eb, :], h2_ref[lanes, :]).astype(BF16)

    def activate(pre, piece, gate_ref, wa_ref):
        lanes = slice(piece * tn, (piece + 1) * tn)
        wa_ref[:, lanes] = jax.nn.gelu(pre) * gate_ref[:, lanes]

    def value_matmul(vt_ref, wa_ref, acc_ref, piece):
        lanes = slice(piece * tn, (piece + 1) * tn)
        acc_ref[:, lanes] += _dot(vt_ref[...], wa_ref[:, lanes])

    @pl.when(j < n_pairs)
    def _():
        stage_rows(0)
        pre0 = pre_act(0, 0)
        gate_tiles(0, gate_a_ref, s_tiles[:4])
        pre1 = pre_act(0, 1)
        gate_tiles(0, gate_a_ref, s_tiles[4:])
        value_matmul(vt_prev_ref, wa_b_ref, acc_b_ref, 0)
        activate(pre0, 0, gate_a_ref, wa_a_ref)
        value_matmul(vt_prev_ref, wa_b_ref, acc_b_ref, 1)
        activate(pre1, 1, gate_a_ref, wa_a_ref)
        stage_rows(1)
        pre0 = pre_act(1, 0)
        gate_tiles(1, gate_b_ref, s_tiles[:4])
        pre1 = pre_act(1, 1)
        gate_tiles(1, gate_b_ref, s_tiles[4:])
        value_matmul(vt_cur_ref, wa_a_ref, acc_a_ref, 0)
        activate(pre0, 0, gate_b_ref, wa_b_ref)
        value_matmul(vt_cur_ref, wa_a_ref, acc_a_ref, 1)
        activate(pre1, 1, gate_b_ref, wa_b_ref)

    @pl.when(j == n_pairs)
    def _():
        nseq, _, _ = m_ref.shape
        d = x_ref.shape[1]
        g2 = m_ref[:, :, 5 * D_MODEL:6 * D_MODEL]
        acc = acc_a_ref[...] + acc_b_ref[...] + _dot(vt_prev_ref[...], wa_b_ref[...])
        out = acc.T.reshape(nseq, tm // nseq, d)
        y = (x_ref[...].reshape(nseq, tm // nseq, d) + g2 * out).reshape(tm, d)
        if final_norm:
            y = _rms(y) * fg_ref[...]
        y_ref[...] = y


def _peer_dense(h2, route, u_b, vt_b, x, mods, final_g, *, tm, eb, final_norm):
    b, t, d = x.shape
    n = b * t
    nseq = max(1, tm // t)
    rank2, e2, n1, e1 = route
    tok = pl.BlockSpec((PEER_HEADS, N_KEYS, tm), lambda i, j: (0, 0, i))
    n_pairs = N_EXPERTS // (2 * eb)
    n_bcast = 2 * PEER_HEADS * (eb // N_KEYS)
    key_rows = pl.BlockSpec((PEER_HEADS, 2 * eb // N_KEYS, tm), lambda i, j: (0, jnp.minimum(j, n_pairs - 1), i))
    y = pl.pallas_call(
        functools.partial(_peer_dense_kernel, eb=eb, final_norm=final_norm),
        out_shape=jax.ShapeDtypeStruct((n, d), F32),
        grid=(n // tm, n_pairs + 1),
        in_specs=[pl.BlockSpec((tm, d), lambda i, j: (i, 0)),
                  pl.BlockSpec((2 * eb, d), lambda i, j: (jnp.minimum(j, n_pairs - 1), 0)),
                  pl.BlockSpec((None, d, eb), lambda i, j: (jnp.maximum(2 * j - 1, 0), 0, 0)),
                  pl.BlockSpec((None, d, eb), lambda i, j: (jnp.minimum(2 * j, 2 * n_pairs - 2), 0, 0)),
                  tok, tok, key_rows, key_rows,
                  pl.BlockSpec((tm, d), lambda i, j: (i, 0)),
                  pl.BlockSpec((nseq, 1, 6 * d), lambda i, j: ((i * tm) // t // nseq, 0, 0)),
                  pl.BlockSpec((1, d), lambda i, j: (0, 0))],
        out_specs=pl.BlockSpec((tm, d), lambda i, j: (i, 0)),
        scratch_shapes=[pltpu.VMEM((d, tm), F32), pltpu.VMEM((d, tm), F32),
                        pltpu.VMEM((eb, tm), BF16), pltpu.VMEM((eb, tm), BF16),
                        pltpu.VMEM((eb, tm), BF16), pltpu.VMEM((eb, tm), BF16),
                        pltpu.VMEM((n_bcast, BF16_ROWS, tm), BF16), pltpu.VMEM((n_bcast, BF16_ROWS, tm), BF16)],
        compiler_params=_cparams("parallel", "arbitrary"),
        name="peer_experts",
    )(h2, u_b, vt_b, vt_b, rank2, e2, n1, e1, x.reshape(n, d), mods, final_g.reshape(1, d))
    return y.reshape(b, t, d)


def _peer(h2, x, mods, w_q, sub_keys, u_b, vt_b, final_g, *, final_norm):
    n = h2.shape[0]
    tm = _row_block(n, 512)
    route = _peer_route(h2, w_q, sub_keys, tm=min(tm, 256))
    return _peer_dense(h2, route, u_b, vt_b, x, mods, final_g, tm=tm, eb=PEER_EXPERT_BLOCK, final_norm=final_norm)


def _latent_kernel(x_ref, g_ref, wc_ref, wr_ref, cs_ref, kvg_ref, c_ref, kr_ref):
    xn = (_rms(x_ref[...]) * g_ref[...]).astype(BF16)
    c_ref[...] = _rms(_dot(xn, wc_ref[...])) * kvg_ref[...]
    p = _dot(xn, wr_ref[...]) * cs_ref[...]
    kr_ref[...] = p[:, :QK_ROPE] + p[:, QK_ROPE:]


def _shared_latent(x, kv_src_g, w_c, w_r2, cs_k, kv_norm_g, *, block_cap=512):
    b, t, d = x.shape
    tm = _row_block(t, block_cap)
    nb = t // tm
    n = b * t
    c_kv, k_r = pl.pallas_call(
        _latent_kernel,
        out_shape=[jax.ShapeDtypeStruct((n, KV_LORA), F32), jax.ShapeDtypeStruct((n, QK_ROPE), F32)],
        grid=(n // tm,),
        in_specs=[pl.BlockSpec((tm, d), lambda i: (i, 0)),
                  pl.BlockSpec((1, d), lambda i: (0, 0)),
                  pl.BlockSpec(w_c.shape, lambda i: (0, 0)),
                  pl.BlockSpec(w_r2.shape, lambda i: (0, 0)),
                  pl.BlockSpec((tm, 2 * QK_ROPE), lambda i: (i % nb, 0)),
                  pl.BlockSpec((1, KV_LORA), lambda i: (0, 0))],
        out_specs=[pl.BlockSpec((tm, KV_LORA), lambda i: (i, 0)), pl.BlockSpec((tm, QK_ROPE), lambda i: (i, 0))],
        compiler_params=_cparams("parallel"),
        name="mla_shared_latent",
    )(x.reshape(n, d), kv_src_g.reshape(1, d), w_c, w_r2, cs_k, kv_norm_g.reshape(1, KV_LORA))
    return c_kv.reshape(b, t, KV_LORA), k_r.reshape(b, t, QK_ROPE)


def _query_kernel(x_ref, m_ref, g_ref, wdq_ref, qg_ref, wq_ref, wqs_ref, c_ref, s_ref, q_ref):
    y = _rms(x_ref[...]) * g_ref[...]
    y = y * (1.0 + m_ref[:, D_MODEL:2 * D_MODEL]) + m_ref[:, 0:D_MODEL]
    cq = (_rms(_dot(y.astype(BF16), wdq_ref[...])) * qg_ref[...]).astype(BF16)
    q = _dot(cq, wq_ref[...])
    qs = _dot(cq, wqs_ref[...])
    cos = c_ref[...]
    sin = s_ref[...]
    for h in range(MLA_HEADS):
        sl = slice(h * QK_PAD, (h + 1) * QK_PAD)
        q_ref[:, sl] = (q[:, sl] * cos + qs[:, sl] * sin).astype(q_ref.dtype)


def _mla_queries(x, mods, norm1_g, w_dq, q_norm_g, w_q, w_qs, cos_q, sin_q, *, block_cap=512):
    b, t, d = x.shape
    tm = _row_block(t, block_cap)
    nb = t // tm
    n = b * t
    q = pl.pallas_call(
        _query_kernel,
        out_shape=jax.ShapeDtypeStruct((n, MLA_HEADS * QK_PAD), BF16),
        grid=(n // tm,),
        in_specs=[pl.BlockSpec((tm, d), lambda i: (i, 0)),
                  pl.BlockSpec((None, 1, 6 * d), lambda i: (i // nb, 0, 0)),
                  pl.BlockSpec((1, d), lambda i: (0, 0)),
                  pl.BlockSpec(w_dq.shape, lambda i: (0, 0)),
                  pl.BlockSpec((1, Q_LORA), lambda i: (0, 0)),
                  pl.BlockSpec(w_q.shape, lambda i: (0, 0)),
                  pl.BlockSpec(w_qs.shape, lambda i: (0, 0)),
                  pl.BlockSpec((tm, QK_PAD), lambda i: (i % nb, 0)),
                  pl.BlockSpec((tm, QK_PAD), lambda i: (i % nb, 0))],
        out_specs=pl.BlockSpec((tm, MLA_HEADS * QK_PAD), lambda i: (i, 0)),
        compiler_params=_cparams("parallel"),
        name="mla_queries",
    )(x.reshape(n, d), mods, norm1_g.reshape(1, d), w_dq, q_norm_g.reshape(1, Q_LORA), w_q, w_qs, cos_q, sin_q)
    return q.reshape(b, t, MLA_HEADS * QK_PAD)


def _kv_kernel(c_ref, kr_ref, wuk_ref, wuv_ref, k_ref, v_ref):
    cb = c_ref[...].astype(BF16)
    kn = _dot(cb, wuk_ref[...])
    vv = _dot(cb, wuv_ref[...])
    tm = c_ref.shape[0]
    tail = jnp.concatenate([kr_ref[...], jnp.zeros((tm, QK_PAD - QK_NOPE - QK_ROPE), F32)], axis=1)
    tail = tail.astype(k_ref.dtype)
    ones_col = (lax.broadcasted_iota(jnp.int32, (tm, V_PAD - V_HEAD), 1) == 0).astype(v_ref.dtype)
    for h in range(MLA_HEADS):
        k_ref[:, h * QK_PAD:h * QK_PAD + QK_NOPE] = kn[:, h * QK_NOPE:(h + 1) * QK_NOPE].astype(k_ref.dtype)
        k_ref[:, h * QK_PAD + QK_NOPE:(h + 1) * QK_PAD] = tail
        v_ref[:, h * V_PAD:h * V_PAD + V_HEAD] = vv[:, h * V_HEAD:(h + 1) * V_HEAD].astype(v_ref.dtype)
        v_ref[:, h * V_PAD + V_HEAD:(h + 1) * V_PAD] = ones_col


def _mla_keys_values(c_all, kr_all, w_uk, w_uv, *, block_cap=512):
    b, tk, _ = c_all.shape
    tm = _row_block(tk, block_cap)
    n = b * tk
    k, v = pl.pallas_call(
        _kv_kernel,
        out_shape=[jax.ShapeDtypeStruct((n, MLA_HEADS * QK_PAD), BF16),
                   jax.ShapeDtypeStruct((n, MLA_HEADS * V_PAD), BF16)],
        grid=(n // tm,),
        in_specs=[pl.BlockSpec((tm, KV_LORA), lambda i: (i, 0)),
                  pl.BlockSpec((tm, QK_ROPE), lambda i: (i, 0)),
                  pl.BlockSpec(w_uk.shape, lambda i: (0, 0)),
                  pl.BlockSpec(w_uv.shape, lambda i: (0, 0))],
        out_specs=[pl.BlockSpec((tm, MLA_HEADS * QK_PAD), lambda i: (i, 0)),
                   pl.BlockSpec((tm, MLA_HEADS * V_PAD), lambda i: (i, 0))],
        compiler_params=_cparams("parallel"),
        name="mla_keys_values",
    )(c_all.reshape(n, KV_LORA), kr_all.reshape(n, QK_ROPE), w_uk, w_uv)
    return k.reshape(b, tk, -1), v.reshape(b, tk, -1)


def _attn_kernel(q_ref, k_ref, v_ref, o_ref, *, tk, pos0, n_keys):
    tq = q_ref.shape[0]
    q = q_ref[...]
    q_pos0 = pos0 + pl.program_id(2) * tq
    q_chunk = (q_pos0 + lax.broadcasted_iota(jnp.int32, (tq, 1), 0)) // CHUNK
    k_end = jnp.minimum(((q_pos0 + tq - 1) // CHUNK + 1) * CHUNK, n_keys)
    n_blocks = (k_end + tk - 1) // tk
    n_full = jnp.minimum((q_pos0 // CHUNK + 1) * CHUNK, n_keys) // tk
    c_exp = SM_SCALE * math.log2(math.e)

    n_split = 4 if tk % (4 * 2 * LANES) == 0 else 1
    th = tk // n_split

    def block(kb, carry, masked):
        m, acc = carry
        starts = [pl.multiple_of(kb * tk + i * th, th) for i in range(n_split)]
        scores = [_dot_nt(q, k_ref[pl.ds(st, th), :]) for st in starts]
        for st, s in zip(starts, scores):
            if masked:
                k_pos = st + lax.broadcasted_iota(jnp.int32, (1, th), 1)
                visible = (k_pos // CHUNK <= q_chunk) & (k_pos < n_keys)
                s = jnp.where(visible, s, -jnp.inf)
            m_new = jnp.maximum(m, jnp.max(s, axis=-1, keepdims=True))
            p = jnp.exp2((s - m_new) * c_exp)
            alpha = jnp.exp2((m - m_new) * c_exp)
            acc = alpha * acc + _dot(p.astype(BF16), v_ref[pl.ds(st, th), :])
            m = m_new
        return m, acc

    init = (jnp.full((tq, 1), -jnp.inf, F32), jnp.zeros((tq, V_PAD), F32))
    carry = lax.fori_loop(0, n_full, functools.partial(block, masked=False), init)
    _, acc = lax.fori_loop(n_full, n_blocks, functools.partial(block, masked=True), carry)
    o_ref[...] = (acc[:, :V_HEAD] / acc[:, V_HEAD:V_HEAD + 1]).astype(o_ref.dtype)


def _attention(q, k, v, *, pos0, n_keys):
    b, t, _ = q.shape
    tkeys = k.shape[1]
    tq = _row_block(t, 256)
    tk = tkeys if tkeys <= ATTN_KEY_BLOCK * 2 + LANES else _row_block(tkeys, ATTN_KEY_BLOCK)
    return pl.pallas_call(
        functools.partial(_attn_kernel, tk=tk, pos0=pos0, n_keys=n_keys),
        out_shape=jax.ShapeDtypeStruct((b, t, MLA_HEADS * V_HEAD), BF16),
        grid=(b, MLA_HEADS, t // tq),
        in_specs=[pl.BlockSpec((None, tq, QK_PAD), lambda bi, h, i: (bi, i, h)),
                  pl.BlockSpec((None, tkeys, QK_PAD), lambda bi, h, i: (bi, 0, h)),
                  pl.BlockSpec((None, tkeys, V_PAD), lambda bi, h, i: (bi, 0, h))],
        out_specs=pl.BlockSpec((None, tq, V_HEAD), lambda bi, h, i: (bi, i, h)),
        compiler_params=_cparams("parallel", "parallel", "arbitrary"),
        name="mla_attention",
    )(q, k, v)


def _rope_tables(pos):
    half = QK_ROPE // 2
    inv = ROPE_THETA ** (-jnp.arange(half, dtype=F32) / half)
    ang = pos.astype(F32)[:, None] * inv[None, :]
    cos, sin = jnp.cos(ang), jnp.sin(ang)
    t = pos.shape[0]
    cs_k = jnp.concatenate([cos, cos, -sin, sin], axis=1)
    one = jnp.ones((t, QK_NOPE), F32)
    zero_n = jnp.zeros((t, QK_NOPE), F32)
    zero_p = jnp.zeros((t, QK_PAD - QK_NOPE - QK_ROPE), F32)
    cos_q = jnp.concatenate([one, cos, cos, zero_p], axis=1)
    sin_q = jnp.concatenate([zero_n, -sin, sin, zero_p], axis=1)
    return cs_k, cos_q, sin_q


def _swap_halves(w):
    half = QK_ROPE // 2
    return jnp.concatenate([w[..., half:], w[..., :half]], axis=-1)


def _prep_weights(p):
    o1 = CONV_DIM + GDN_HEADS * GDN_DV
    w = {}
    w_in = p['gdn_w_in']
    w['gdn_qkv'] = w_in[:, :, :CONV_DIM].astype(BF16)
    w['gdn_gate'] = w_in[:, :, CONV_DIM:o1].astype(BF16)
    w['gdn_ab'] = jnp.pad(w_in[:, :, o1:], ((0, 0), (0, 0), (0, LANES - 2 * GDN_HEADS))).astype(BF16)
    w['gdn_o'] = p['gdn_w_o'].astype(BF16)
    w_dkv = p['mla_w_dkv']
    w['dkv_c'] = w_dkv[:, :KV_LORA].astype(BF16)
    rope_cols = w_dkv[:, KV_LORA:]
    w['dkv_r2'] = jnp.concatenate([rope_cols, _swap_halves(rope_cols)], axis=1).astype(BF16)
    w['uk'] = p['mla_w_uk'].astype(BF16)
    w['uv'] = p['mla_w_uv'].astype(BF16)
    w['dq'] = p['mla_w_dq'].astype(BF16)
    n_b = p['mla_w_uq'].shape[0]
    uq = p['mla_w_uq'].reshape(n_b, Q_LORA, MLA_HEADS, QK_NOPE + QK_ROPE)
    pad = jnp.zeros((n_b, Q_LORA, MLA_HEADS, QK_PAD - QK_NOPE - QK_ROPE), F32)
    w['uq'] = jnp.concatenate([uq, pad], axis=-1).reshape(n_b, Q_LORA, MLA_HEADS * QK_PAD).astype(BF16)
    uq_s = jnp.concatenate([jnp.zeros_like(uq[..., :QK_NOPE]), _swap_halves(uq[..., QK_NOPE:]), pad], axis=-1)
    w['uq_swap'] = uq_s.reshape(n_b, Q_LORA, MLA_HEADS * QK_PAD).astype(BF16)
    w['mla_o'] = p['mla_w_o'].astype(BF16)
    w['peer_q'] = p['peer_w_q'].astype(BF16)
    w['peer_keys'] = p['peer_sub_keys'].astype(BF16)
    w['peer_u'] = p['peer_u'].astype(BF16)
    pv = p['peer_v'].astype(BF16)
    pv = pv.reshape(pv.shape[0], N_EXPERTS // PEER_EXPERT_BLOCK, PEER_EXPERT_BLOCK, pv.shape[-1])
    w['peer_vt'] = jnp.swapaxes(pv, 2, 3)
    return w


def _trunk(x, mods, pos0, conv_bufs, delta_states, past_c, past_kr, p, w):
    b, t, d = x.shape
    depth = mods.shape[0]
    n_a = p['gdn_w_in'].shape[0]
    new_states, new_bufs = [], []
    c_kv = k_r = k_cat = v_all = None
    n_keys = past_c.shape[1] + t
    for l in range(depth):
        m = mods[l]
        if l < n_a:
            qkv, gate, ab = _norm_matmul(x, m, p['norm1_g'][l], [w['gdn_qkv'][l], w['gdn_gate'][l], w['gdn_ab'][l]],
                                         [F32, F32, F32], mod_off=0)
            o, s = _gdn(qkv, gate, ab, conv_bufs[l], delta_states[l], p['gdn_conv_w'][l], p['gdn_a_log'][l],
                        p['gdn_dt_bias'][l], p['gdn_onorm_g'][l])
            new_bufs.append(qkv[:, t - (CONV_W - 1):, :])
            new_states.append(s)
            w_o = w['gdn_o'][l]
        else:
            j = l - n_a
            cs_k, cos_q, sin_q = _rope_tables(pos0 + jnp.arange(t))
            if j == 0:
                c_kv, k_r = _shared_latent(x, p['kv_src_g'], w['dkv_c'], w['dkv_r2'], cs_k, p['mla_kv_norm_g'])
                c_all = jnp.concatenate([past_c, c_kv], axis=1)
                kr_all = jnp.concatenate([past_kr, k_r], axis=1)
                pad = -n_keys % LANES
                c_all = jnp.pad(c_all, ((0, 0), (0, pad), (0, 0)))
                kr_all = jnp.pad(kr_all, ((0, 0), (0, pad), (0, 0)))
                k_cat, v_all = _mla_keys_values(c_all, kr_all, w['uk'], w['uv'])
            q = _mla_queries(x, m, p['norm1_g'][l], w['dq'][j], p['mla_q_norm_g'][j], w['uq'][j], w['uq_swap'][j],
                             cos_q, sin_q)
            o = _attention(q, k_cat, v_all, pos0=pos0, n_keys=n_keys)
            w_o = w['mla_o'][j]
        x, h2 = _out_residual(o, w_o, x, m, p['norm2_g'][l])
        x = _peer(h2, x, m, w['peer_q'][l], w['peer_keys'][l], w['peer_u'][l], w['peer_vt'][l], p['final_g'],
                  final_norm=(l == depth - 1))
    return x, jnp.stack(new_states), jnp.stack(new_bufs), c_kv, k_r


def kernel(x_prompt, x_sample, c_prompt, c_sample, state_delta, state_conv, cache_kv_latent, cache_k_rope, w_mod, b_mod, norm1_g, norm2_g, final_g, gdn_w_in, gdn_conv_w, gdn_a_log, gdn_dt_bias, gdn_onorm_g, gdn_w_o, kv_src_g, mla_w_dkv, mla_kv_norm_g, mla_w_uk, mla_w_uv, mla_w_dq, mla_q_norm_g, mla_w_uq, mla_w_o, peer_w_q, peer_sub_keys, peer_u, peer_v):
    p = dict(norm1_g=norm1_g, norm2_g=norm2_g, final_g=final_g, gdn_w_in=gdn_w_in, gdn_conv_w=gdn_conv_w,
             gdn_a_log=gdn_a_log, gdn_dt_bias=gdn_dt_bias, gdn_onorm_g=gdn_onorm_g, gdn_w_o=gdn_w_o,
             kv_src_g=kv_src_g, mla_w_dkv=mla_w_dkv, mla_kv_norm_g=mla_kv_norm_g, mla_w_uk=mla_w_uk,
             mla_w_uv=mla_w_uv, mla_w_dq=mla_w_dq, mla_q_norm_g=mla_q_norm_g, mla_w_uq=mla_w_uq, mla_w_o=mla_w_o,
             peer_w_q=peer_w_q, peer_sub_keys=peer_sub_keys, peer_u=peer_u, peer_v=peer_v)
    w = _prep_weights(p)
    bp, tp, d = x_prompt.shape
    bs, ts, _ = x_sample.shape
    n_a = gdn_w_in.shape[0]

    rows = -(-(bp + bs) // SUBLANES) * SUBLANES
    c_all = jnp.pad(jnp.concatenate([c_prompt, c_sample], axis=0), ((0, rows - bp - bs), (0, 0)))
    mods = _modulation(c_all, w_mod, b_mod)
    mods_p = mods[:, :bp, None, :]
    mods_s = mods[:, bp:bp + bs, None, :]

    zero_bufs = jnp.zeros((n_a, bp, CONV_W - 1, CONV_DIM), F32)
    zero_states = jnp.zeros((n_a, bp, GDN_HEADS, GDN_DK, GDN_DV), F32)
    no_c = jnp.zeros((bp, 0, KV_LORA), F32)
    no_kr = jnp.zeros((bp, 0, QK_ROPE), F32)
    y_p, p_delta, p_conv, p_lat, p_kr = _trunk(x_prompt, mods_p, 0, zero_bufs, zero_states, no_c, no_kr, p, w)
    past = cache_kv_latent.shape[1]
    y_s, s_delta, s_conv, s_lat, s_kr = _trunk(x_sample, mods_s, past, state_conv, state_delta, cache_kv_latent,
                                               cache_k_rope, p, w)
    return (y_p, y_s, p_delta, p_conv, p_lat, p_kr, s_delta, s_conv, s_lat, s_kr)
```

```python
import functools
import math

import jax
import jax.numpy as jnp
from jax import lax
from jax.experimental import pallas as pl
from jax.experimental.pallas import tpu as pltpu

F32 = jnp.float32
BF16 = jnp.bfloat16
HIGHEST = lax.Precision.HIGHEST

D_MODEL = 1024
EPS = 1e-6
CHUNK = 64

GDN_HEADS = 8
GDN_DK = 128
GDN_DV = 128
CONV_W = 4
GDN_QK = GDN_HEADS * GDN_DK
CONV_DIM = GDN_HEADS * (2 * GDN_DK + GDN_DV)

MLA_HEADS = 8
QK_NOPE = 128
QK_ROPE = 64
V_HEAD = 128
KV_LORA = 256
Q_LORA = 384
ROPE_THETA = 10000.0
SM_SCALE = (QK_NOPE + QK_ROPE) ** -0.5
QK_PAD = 256
V_PAD = 256
ATTN_KEY_BLOCK = 1024

N_KEYS = 128
N_EXPERTS = N_KEYS * N_KEYS
PEER_HEADS = 8
PEER_TOPK = 16
PK_HALF = 128
PEER_EXPERT_BLOCK = 512

LANES = 128
SUBLANES = 8
BF16_ROWS = 16
VMEM_LIMIT = 56 * 1024 * 1024


def _cparams(*sem):
    return pltpu.CompilerParams(dimension_semantics=sem, vmem_limit_bytes=VMEM_LIMIT)


def _silu(x):
    return x * jax.nn.sigmoid(x)


def _softplus(x):
    return jnp.maximum(x, 0.0) + jnp.log1p(jnp.exp(-jnp.abs(x)))


def _rms(x):
    return x * lax.rsqrt(jnp.mean(x * x, axis=-1, keepdims=True) + EPS)


def _dot(a, b, **kw):
    return jnp.dot(a, b, preferred_element_type=F32, **kw)


def _dot_nt(a, b, **kw):
    return lax.dot_general(a, b, (((1,), (1,)), ((), ())), preferred_element_type=F32, **kw)


def _dot_tn(a, b, **kw):
    return lax.dot_general(a, b, (((0,), (0,)), ((), ())), preferred_element_type=F32, **kw)


def _split_bf16(a):
    hi = a.astype(BF16)
    return hi, (a - hi.astype(F32)).astype(BF16)


def _dot3(a, b):
    ah, al = _split_bf16(a)
    bh, bl = _split_bf16(b)
    return _dot(jnp.concatenate([ah, ah, al], axis=1), jnp.concatenate([bh, bl, bh], axis=0))


def _row_block(t, cap):
    b = min(t, cap)
    while t % b:
        b //= 2
    return b


def _mod_kernel(c_ref, w_ref, b_ref, o_ref):
    a = _silu(c_ref[...]).astype(BF16)
    o_ref[...] = _dot(a, w_ref[...].astype(BF16)) + b_ref[...]


def _modulation(c_pad, w_mod, b_mod):
    depth, d, n = w_mod.shape
    r = c_pad.shape[0]
    tn = 1024
    return pl.pallas_call(
        _mod_kernel,
        out_shape=jax.ShapeDtypeStruct((depth, r, n), F32),
        grid=(depth, n // tn),
        in_specs=[pl.BlockSpec((r, d), lambda l, j: (0, 0)),
                  pl.BlockSpec((None, d, tn), lambda l, j: (l, 0, j)),
                  pl.BlockSpec((None, 1, tn), lambda l, j: (l, 0, j))],
        out_specs=pl.BlockSpec((None, r, tn), lambda l, j: (l, 0, j)),
        compiler_params=_cparams("arbitrary", "arbitrary"),
        name="adaln_modulation",
    )(c_pad, w_mod, b_mod.reshape(depth, 1, n))


def _norm_mm_kernel(*refs, n_w, mod_off):
    x_ref, m_ref, g_ref = refs[:3]
    w_refs = refs[3:3 + n_w]
    o_refs = refs[3 + n_w:]
    y = _rms(x_ref[...]) * g_ref[...]
    if mod_off is not None:
        sh = m_ref[:, mod_off * D_MODEL:(mod_off + 1) * D_MODEL]
        sc = m_ref[:, (mod_off + 1) * D_MODEL:(mod_off + 2) * D_MODEL]
        y = y * (1.0 + sc) + sh
    yb = y.astype(BF16)
    for w_ref, o_ref in zip(w_refs, o_refs):
        o_ref[...] = _dot(yb, w_ref[...]).astype(o_ref.dtype)


def _norm_matmul(x, mods, g, weights, out_dtypes, *, mod_off, block_cap=512):
    b, t, d = x.shape
    tm = _row_block(t, block_cap)
    nb = t // tm
    n = b * t
    x2 = x.reshape(n, d)
    in_specs = [pl.BlockSpec((tm, d), lambda i: (i, 0)),
                pl.BlockSpec((None, 1, 6 * d), lambda i: (i // nb, 0, 0)),
                pl.BlockSpec((1, d), lambda i: (0, 0))]
    in_specs += [pl.BlockSpec(w.shape, lambda i: (0, 0)) for w in weights]
    outs = pl.pallas_call(
        functools.partial(_norm_mm_kernel, n_w=len(weights), mod_off=mod_off),
        out_shape=[jax.ShapeDtypeStruct((n, w.shape[1]), dt) for w, dt in zip(weights, out_dtypes)],
        grid=(n // tm,),
        in_specs=in_specs,
        out_specs=[pl.BlockSpec((tm, w.shape[1]), lambda i: (i, 0)) for w in weights],
        compiler_params=_cparams("parallel"),
        name="norm_matmul",
    )(x2, mods, g.reshape(1, d), *weights)
    return [o.reshape(b, t, o.shape[-1]) for o in outs]


def _gdn_kernel(qkv_ref, gate_ref, ab_ref, cbuf_ref, s0_ref, cw_ref, par_ref, og_ref,
                o_ref, s_ref, xp_ref, *, chunk):
    c = chunk
    step = pl.program_id(1)

    @pl.when(step == 0)
    def _():
        s_ref[...] = s0_ref[...]
        xp_ref[0:SUBLANES, :] = cbuf_ref[...]

    xp_ref[SUBLANES:SUBLANES + c, :] = qkv_ref[...]
    base = SUBLANES - (CONV_W - 1)
    conv = xp_ref[base:base + c, :] * cw_ref[0:1, :]
    for w in range(1, CONV_W):
        conv = conv + xp_ref[base + w:base + w + c, :] * cw_ref[w:w + 1, :]
    xp_ref[0:SUBLANES, :] = xp_ref[c:c + SUBLANES, :]
    act = _silu(conv)

    ab = ab_ref[...]
    g_all = -jnp.exp(par_ref[0:1, :]) * _softplus(ab + par_ref[1:2, :])
    beta_all = jax.nn.sigmoid(ab)

    row = lax.broadcasted_iota(jnp.int32, (c, c), 0)
    col = lax.broadcasted_iota(jnp.int32, (c, c), 1)
    causal = row >= col
    strict = row > col
    eye = (row == col).astype(F32)
    lower = causal.astype(F32)
    upper = (row <= col).astype(F32)
    r128 = lax.broadcasted_iota(jnp.int32, (LANES, LANES), 0)
    c128 = lax.broadcasted_iota(jnp.int32, (LANES, LANES), 1)
    eye128 = (r128 == c128).astype(F32)

    d_col_all = _dot(lower, g_all, precision=HIGHEST)
    g_t = _dot_nt(eye128, g_all, precision=HIGHEST)
    d_row_all = _dot(g_t, upper, precision=HIGHEST)

    n_double = int(math.log2(c)) - 1
    keep_s = lax.broadcasted_iota(jnp.int32, (c, 2 * c), 1) < c
    heads = range(GDN_HEADS)
    q, k, v, beta, d_col, d_last, decay, e_col, kb, kq = ([None] * GDN_HEADS for _ in range(10))
    for h in heads:
        qh = act[:, h * GDN_DK:(h + 1) * GDN_DK]
        kh = act[:, GDN_QK + h * GDN_DK:GDN_QK + (h + 1) * GDN_DK]
        v[h] = act[:, 2 * GDN_QK + h * GDN_DV:2 * GDN_QK + (h + 1) * GDN_DV]
        q[h] = qh * lax.rsqrt(jnp.sum(qh * qh, axis=-1, keepdims=True) + EPS) * (GDN_DK ** -0.5)
        k[h] = kh * lax.rsqrt(jnp.sum(kh * kh, axis=-1, keepdims=True) + EPS)
        beta[h] = beta_all[:, GDN_HEADS + h:GDN_HEADS + h + 1]
        d_col[h] = d_col_all[:, h:h + 1]
        d_last[h] = d_col_all[c - 1:c, h:h + 1]
        decay[h] = jnp.exp(jnp.where(causal, d_col[h] - d_row_all[h:h + 1, :], -jnp.inf))
        e_col[h] = jnp.exp(d_col[h])
        kb[h] = k[h] * beta[h]
    for h in heads:
        kq[h] = _dot_nt(jnp.concatenate([kb[h], q[h]], axis=0).astype(BF16), k[h].astype(BF16))
    m = [-jnp.where(strict, kq[h][:c] * decay[h], 0.0) for h in heads]
    x = [jnp.concatenate([eye, m[h]], axis=1) for h in heads]
    for _ in range(n_double):
        for h in heads:
            x[h] = jnp.where(keep_s, x[h], 0.0) + _dot3(m[h], x[h])
            m[h] = x[h][:, c:]
    p = [x[h][:, :c] + _dot3(m[h], x[h][:, :c]) for h in heads]
    sol = [_dot3(p[h], jnp.concatenate([v[h] * beta[h], kb[h] * e_col[h]], axis=1)) for h in heads]
    s = [s_ref[h] for h in heads]
    ws = [_dot(jnp.concatenate([sol[h][:, GDN_DV:], q[h] * e_col[h]], axis=0).astype(BF16), s[h].astype(BF16))
          for h in heads]
    vnb = [(sol[h][:, :GDN_DV] - ws[h][:c]).astype(BF16) for h in heads]
    o = [ws[h][c:] + _dot((kq[h][c:] * decay[h]).astype(BF16), vnb[h]) for h in heads]
    for h in heads:
        k_dec = (k[h] * jnp.exp(d_last[h] - d_col[h])).astype(BF16)
        s_ref[h] = s[h] * jnp.exp(d_last[h]) + _dot_tn(k_dec, vnb[h])
    for h in heads:
        gate = gate_ref[:, h * GDN_DV:(h + 1) * GDN_DV]
        o_ref[:, h * GDN_DV:(h + 1) * GDN_DV] = (_rms(o[h]) * og_ref[...] * _silu(gate)).astype(o_ref.dtype)


def _gdn(qkv, gate, ab, conv_buf, s0, conv_w, a_log, dt_bias, onorm_g):
    b, t, _ = qkv.shape
    c = min(CHUNK, t)
    cbuf = jnp.pad(conv_buf, ((0, 0), (SUBLANES - (CONV_W - 1), 0), (0, 0)))
    cw = jnp.pad(conv_w, ((0, SUBLANES - CONV_W), (0, 0)))
    par = jnp.zeros((SUBLANES, LANES), F32)
    par = par.at[0, :GDN_HEADS].set(a_log).at[1, :GDN_HEADS].set(dt_bias)
    o, s = pl.pallas_call(
        functools.partial(_gdn_kernel, chunk=c),
        out_shape=[jax.ShapeDtypeStruct((b, t, GDN_HEADS * GDN_DV), BF16),
                   jax.ShapeDtypeStruct((b, GDN_HEADS, GDN_DK, GDN_DV), F32)],
        grid=(b, t // c),
        in_specs=[pl.BlockSpec((None, c, CONV_DIM), lambda i, j: (i, j, 0)),
                  pl.BlockSpec((None, c, GDN_HEADS * GDN_DV), lambda i, j: (i, j, 0)),
                  pl.BlockSpec((None, c, LANES), lambda i, j: (i, j, 0)),
                  pl.BlockSpec((None, SUBLANES, CONV_DIM), lambda i, j: (i, 0, 0)),
                  pl.BlockSpec((None, GDN_HEADS, GDN_DK, GDN_DV), lambda i, j: (i, 0, 0, 0)),
                  pl.BlockSpec((SUBLANES, CONV_DIM), lambda i, j: (0, 0)),
                  pl.BlockSpec((SUBLANES, LANES), lambda i, j: (0, 0)),
                  pl.BlockSpec((1, GDN_DV), lambda i, j: (0, 0))],
        out_specs=[pl.BlockSpec((None, c, GDN_HEADS * GDN_DV), lambda i, j: (i, j, 0)),
                   pl.BlockSpec((None, GDN_HEADS, GDN_DK, GDN_DV), lambda i, j: (i, 0, 0, 0))],
        scratch_shapes=[pltpu.VMEM((c + SUBLANES, CONV_DIM), F32)],
        compiler_params=_cparams("parallel", "arbitrary"),
        name="gated_delta",
    )(qkv, gate, ab, cbuf, s0, cw, par, onorm_g.reshape(1, GDN_DV))
    return o, s


def _out_res_kernel(o_ref, w_ref, x_ref, m_ref, g2_ref, y_ref, h2_ref):
    def mod(i):
        return m_ref[:, i * D_MODEL:(i + 1) * D_MODEL]

    y = x_ref[...] + mod(2) * _dot(o_ref[...], w_ref[...])
    y_ref[...] = y
    h2_ref[...] = (_rms(y) * g2_ref[...] * (1.0 + mod(4)) + mod(3)).astype(h2_ref.dtype)


def _out_residual(o, w, x, mods, norm2_g, *, block_cap=512):
    b, t, d = x.shape
    kdim = o.shape[-1]
    tm = _row_block(t, block_cap)
    nb = t // tm
    n = b * t
    y, h2 = pl.pallas_call(
        _out_res_kernel,
        out_shape=[jax.ShapeDtypeStruct((n, d), F32), jax.ShapeDtypeStruct((n, d), BF16)],
        grid=(n // tm,),
        in_specs=[pl.BlockSpec((tm, kdim), lambda i: (i, 0)),
                  pl.BlockSpec((kdim, d), lambda i: (0, 0)),
                  pl.BlockSpec((tm, d), lambda i: (i, 0)),
                  pl.BlockSpec((None, 1, 6 * d), lambda i: (i // nb, 0, 0)),
                  pl.BlockSpec((1, d), lambda i: (0, 0))],
        out_specs=[pl.BlockSpec((tm, d), lambda i: (i, 0)), pl.BlockSpec((tm, d), lambda i: (i, 0))],
        compiler_params=_cparams("parallel"),
        name="out_proj_residual",
    )(o.reshape(n, kdim), w, x.reshape(n, d), mods, norm2_g.reshape(1, d))
    return y.reshape(b, t, d), h2


_CAND = [(j1, j2) for j1 in range(PEER_TOPK) for j2 in range(PEER_TOPK) if (j1 + 1) * (j2 + 1) <= PEER_TOPK]
_CAND_PAD = -(-len(_CAND) // SUBLANES) * SUBLANES


def _top_values(s, n_top):
    vals = [jnp.max(s, axis=0, keepdims=True)]
    for _ in range(1, n_top):
        vals.append(jnp.max(jnp.where(s < vals[-1], s, -jnp.inf), axis=0, keepdims=True))
    return vals


def _rank_among(s, vals):
    rank = jnp.zeros(s.shape, F32)
    for v in vals:
        rank = rank + jnp.where(s < v, 1.0, 0.0)
    return rank


def _peer_route_kernel(h2_ref, wq_ref, keys_ref, rank2_ref, e2_ref, n1_ref, e1_ref, q_scr):
    q_scr[...] = _dot(h2_ref[...], wq_ref[...]).astype(BF16)
    tm = h2_ref.shape[0]

    def head(h, carry):
        off = pl.multiple_of(h * 2 * PK_HALF, 2 * PK_HALF)
        s1 = _dot_nt(keys_ref[h, 0], q_scr[:, pl.ds(off, PK_HALF)])
        s2 = _dot_nt(keys_ref[h, 1], q_scr[:, pl.ds(off + PK_HALF, PK_HALF)])
        v1 = _top_values(s1, PEER_TOPK)
        v2 = _top_values(s2, PEER_TOPK)
        rank2 = _rank_among(s2, v2)
        rows = [v1[j1] + v2[j2] for j1, j2 in _CAND]
        rows += [jnp.full((1, tm), -jnp.inf, F32)] * (_CAND_PAD - len(_CAND))
        cand = jnp.concatenate(rows, axis=0)
        tau = _top_values(cand, PEER_TOPK)[-1]
        sel = jnp.where(cand >= tau, 1.0, 0.0)
        z = jnp.sum(sel * jnp.exp(cand - cand[0:1, :]), axis=0, keepdims=True)
        n1 = jnp.zeros(s1.shape, F32)
        off_c = 0
        for j1 in range(PEER_TOPK):
            cnt = PEER_TOPK // (j1 + 1)
            n_j = jnp.sum(sel[off_c:off_c + cnt, :], axis=0, keepdims=True)
            n1 = jnp.where(s1 == v1[j1], n_j, n1)
            off_c += cnt
        rank2_ref[h] = rank2.astype(rank2_ref.dtype)
        e2_ref[h] = jnp.exp(s2 - v2[0]).astype(e2_ref.dtype)
        n1_ref[h] = n1
        e1_ref[h] = jnp.exp(s1 - v1[0]) * (1.0 / z)
        return carry

    lax.fori_loop(0, PEER_HEADS, head, 0)


def _peer_route(h2, w_q, sub_keys, *, tm):
    n, d = h2.shape
    shp = (PEER_HEADS, N_KEYS, n)
    blk = pl.BlockSpec((PEER_HEADS, N_KEYS, tm), lambda i: (0, 0, i))
    return pl.pallas_call(
        _peer_route_kernel,
        out_shape=[jax.ShapeDtypeStruct(shp, BF16), jax.ShapeDtypeStruct(shp, BF16),
                   jax.ShapeDtypeStruct(shp, F32), jax.ShapeDtypeStruct(shp, F32)],
        grid=(n // tm,),
        in_specs=[pl.BlockSpec((tm, d), lambda i: (i, 0)),
                  pl.BlockSpec(w_q.shape, lambda i: (0, 0)),
                  pl.BlockSpec(sub_keys.shape, lambda i: (0, 0, 0, 0))],
        out_specs=[blk, blk, blk, blk],
        scratch_shapes=[pltpu.VMEM((tm, w_q.shape[1]), BF16)],
        compiler_params=_cparams("parallel"),
        name="peer_route",
    )(h2, w_q, sub_keys)


def _peer_dense_kernel(h2_ref, u_ref, vt_prev_ref, vt_cur_ref, rank2_ref, e2_ref, n1_ref, e1_ref, x_ref, m_ref,
                       fg_ref, y_ref, acc_a_ref, acc_b_ref, wa_a_ref, wa_b_ref, gate_a_ref, gate_b_ref, nb_ref, eb_ref,
                       *, eb, final_norm):
    j = pl.program_id(1)
    n_pairs = pl.num_programs(1) - 1
    rows_per_half = eb // N_KEYS
    tm = h2_ref.shape[0]
    zero = jnp.zeros((), BF16)

    @pl.when(j == 0)
    def _():
        acc_a_ref[...] = jnp.zeros_like(acc_a_ref)
        acc_b_ref[...] = jnp.zeros_like(acc_b_ref)
        wa_b_ref[...] = jnp.zeros_like(wa_b_ref)

    tn = tm // 2
    s_tiles = list(range(0, N_KEYS, BF16_ROWS))

    def stage_rows(half):
        for h in range(PEER_HEADS):
            n_all = n1_ref[h].astype(BF16)
            e_all = e1_ref[h].astype(BF16)
            for r in range(rows_per_half):
                k = (half * PEER_HEADS + h) * rows_per_half + r
                row = half * rows_per_half + r
                nb_ref[k] = jnp.broadcast_to(n_all[row:row + 1, :], (BF16_ROWS, tm))
                eb_ref[k] = jnp.broadcast_to(e_all[row:row + 1, :], (BF16_ROWS, tm))

    def gate_tiles(half, gate_ref, tiles):
        for s0 in tiles:
            w = [None] * rows_per_half
            for h in range(PEER_HEADS):
                r2 = rank2_ref[h, s0:s0 + BF16_ROWS, :]
                e2 = e2_ref[h, s0:s0 + BF16_ROWS, :]
                for r in range(rows_per_half):
                    k = (half * PEER_HEADS + h) * rows_per_half + r
                    term = jnp.where(r2 < nb_ref[k], e2, zero) * eb_ref[k]
                    w[r] = term if w[r] is None else w[r] + term
            for r in range(rows_per_half):
                gate_ref[r * N_KEYS + s0:r * N_KEYS + s0 + BF16_ROWS, :] = w[r]

    def pre_act(half, piece):
        lanes = slice(piece * tn, (piece + 1) * tn)
        return _dot_nt(u_ref[half * eb:(half + 1) * eb, :], h2_ref[lanes, :]).astype(BF16)

    def activate(pre, piece, gate_ref, wa_ref):
        lanes = slice(piece * tn, (piece + 1) * tn)
        wa_ref[:, lanes] = jax.nn.gelu(pre) * gate_ref[:, lanes]

    def value_matmul(vt_ref, wa_ref, acc_ref, piece):
        lanes = slice(piece * tn, (piece + 1) * tn)
        acc_ref[:, lanes] += _dot(vt_ref[...], wa_ref[:, lanes])

    @pl.when(j < n_pairs)
    def _():
        stage_rows(0)
        stage_rows(1)
        pre0 = pre_act(0, 0)
        gate_tiles(0, gate_a_ref, s_tiles[:4])
        pre1 = pre_act(0, 1)
        gate_tiles(0, gate_a_ref, s_tiles[4:])
        value_matmul(vt_prev_ref, wa_b_ref, acc_b_ref, 0)
        activate(pre0, 0, gate_a_ref, wa_a_ref)
        value_matmul(vt_prev_ref, wa_b_ref, acc_b_ref, 1)
        activate(pre1, 1, gate_a_ref, wa_a_ref)
        pre0 = pre_act(1, 0)
        gate_tiles(1, gate_b_ref, s_tiles[:4])
        pre1 = pre_act(1, 1)
        gate_tiles(1, gate_b_ref, s_tiles[4:])
        value_matmul(vt_cur_ref, wa_a_ref, acc_a_ref, 0)
        activate(pre0, 0, gate_b_ref, wa_b_ref)
        value_matmul(vt_cur_ref, wa_a_ref, acc_a_ref, 1)
        activate(pre1, 1, gate_b_ref, wa_b_ref)

    @pl.when(j == n_pairs)
    def _():
        nseq, _, _ = m_ref.shape
        d = x_ref.shape[1]
        g2 = m_ref[:, :, 5 * D_MODEL:6 * D_MODEL]
        acc = acc_a_ref[...] + acc_b_ref[...] + _dot(vt_prev_ref[...], wa_b_ref[...])
        out = acc.T.reshape(nseq, tm // nseq, d)
        y = (x_ref[...].reshape(nseq, tm // nseq, d) + g2 * out).reshape(tm, d)
        if final_norm:
            y = _rms(y) * fg_ref[...]
        y_ref[...] = y


def _peer_dense(h2, route, u_b, vt_b, x, mods, final_g, *, tm, eb, final_norm):
    b, t, d = x.shape
    n = b * t
    nseq = max(1, tm // t)
    rank2, e2, n1, e1 = route
    tok = pl.BlockSpec((PEER_HEADS, N_KEYS, tm), lambda i, j: (0, 0, i))
    n_pairs = N_EXPERTS // (2 * eb)
    n_bcast = 2 * PEER_HEADS * (eb // N_KEYS)
    key_rows = pl.BlockSpec((PEER_HEADS, 2 * eb // N_KEYS, tm), lambda i, j: (0, jnp.minimum(j, n_pairs - 1), i))
    y = pl.pallas_call(
        functools.partial(_peer_dense_kernel, eb=eb, final_norm=final_norm),
        out_shape=jax.ShapeDtypeStruct((n, d), F32),
        grid=(n // tm, n_pairs + 1),
        in_specs=[pl.BlockSpec((tm, d), lambda i, j: (i, 0)),
                  pl.BlockSpec((2 * eb, d), lambda i, j: (jnp.minimum(j, n_pairs - 1), 0)),
                  pl.BlockSpec((None, d, eb), lambda i, j: (jnp.maximum(2 * j - 1, 0), 0, 0)),
                  pl.BlockSpec((None, d, eb), lambda i, j: (jnp.minimum(2 * j, 2 * n_pairs - 2), 0, 0)),
                  tok, tok, key_rows, key_rows,
                  pl.BlockSpec((tm, d), lambda i, j: (i, 0)),
                  pl.BlockSpec((nseq, 1, 6 * d), lambda i, j: ((i * tm) // t // nseq, 0, 0)),
                  pl.BlockSpec((1, d), lambda i, j: (0, 0))],
        out_specs=pl.BlockSpec((tm, d), lambda i, j: (i, 0)),
        scratch_shapes=[pltpu.VMEM((d, tm), F32), pltpu.VMEM((d, tm), F32),
                        pltpu.VMEM((eb, tm), BF16), pltpu.VMEM((eb, tm), BF16),
                        pltpu.VMEM((eb, tm), BF16), pltpu.VMEM((eb, tm), BF16),
                        pltpu.VMEM((n_bcast, BF16_ROWS, tm), BF16), pltpu.VMEM((n_bcast, BF16_ROWS, tm), BF16)],
        compiler_params=_cparams("parallel", "arbitrary"),
        name="peer_experts",
    )(h2, u_b, vt_b, vt_b, rank2, e2, n1, e1, x.reshape(n, d), mods, final_g.reshape(1, d))
    return y.reshape(b, t, d)


def _peer(h2, x, mods, w_q, sub_keys, u_b, vt_b, final_g, *, final_norm):
    n = h2.shape[0]
    tm = _row_block(n, 512)
    route = _peer_route(h2, w_q, sub_keys, tm=min(tm, 256))
    return _peer_dense(h2, route, u_b, vt_b, x, mods, final_g, tm=tm, eb=PEER_EXPERT_BLOCK, final_norm=final_norm)


def _latent_kernel(x_ref, g_ref, wc_ref, wr_ref, cs_ref, kvg_ref, c_ref, kr_ref):
    xn = (_rms(x_ref[...]) * g_ref[...]).astype(BF16)
    c_ref[...] = _rms(_dot(xn, wc_ref[...])) * kvg_ref[...]
    p = _dot(xn, wr_ref[...]) * cs_ref[...]
    kr_ref[...] = p[:, :QK_ROPE] + p[:, QK_ROPE:]


def _shared_latent(x, kv_src_g, w_c, w_r2, cs_k, kv_norm_g, *, block_cap=512):
    b, t, d = x.shape
    tm = _row_block(t, block_cap)
    nb = t // tm
    n = b * t
    c_kv, k_r = pl.pallas_call(
        _latent_kernel,
        out_shape=[jax.ShapeDtypeStruct((n, KV_LORA), F32), jax.ShapeDtypeStruct((n, QK_ROPE), F32)],
        grid=(n // tm,),
        in_specs=[pl.BlockSpec((tm, d), lambda i: (i, 0)),
                  pl.BlockSpec((1, d), lambda i: (0, 0)),
                  pl.BlockSpec(w_c.shape, lambda i: (0, 0)),
                  pl.BlockSpec(w_r2.shape, lambda i: (0, 0)),
                  pl.BlockSpec((tm, 2 * QK_ROPE), lambda i: (i % nb, 0)),
                  pl.BlockSpec((1, KV_LORA), lambda i: (0, 0))],
        out_specs=[pl.BlockSpec((tm, KV_LORA), lambda i: (i, 0)), pl.BlockSpec((tm, QK_ROPE), lambda i: (i, 0))],
        compiler_params=_cparams("parallel"),
        name="mla_shared_latent",
    )(x.reshape(n, d), kv_src_g.reshape(1, d), w_c, w_r2, cs_k, kv_norm_g.reshape(1, KV_LORA))
    return c_kv.reshape(b, t, KV_LORA), k_r.reshape(b, t, QK_ROPE)


def _query_kernel(x_ref, m_ref, g_ref, wdq_ref, qg_ref, wq_ref, wqs_ref, c_ref, s_ref, q_ref):
    y = _rms(x_ref[...]) * g_ref[...]
    y = y * (1.0 + m_ref[:, D_MODEL:2 * D_MODEL]) + m_ref[:, 0:D_MODEL]
    cq = (_rms(_dot(y.astype(BF16), wdq_ref[...])) * qg_ref[...]).astype(BF16)
    q = _dot(cq, wq_ref[...])
    qs = _dot(cq, wqs_ref[...])
    cos = c_ref[...]
    sin = s_ref[...]
    for h in range(MLA_HEADS):
        sl = slice(h * QK_PAD, (h + 1) * QK_PAD)
        q_ref[:, sl] = (q[:, sl] * cos + qs[:, sl] * sin).astype(q_ref.dtype)


def _mla_queries(x, mods, norm1_g, w_dq, q_norm_g, w_q, w_qs, cos_q, sin_q, *, block_cap=512):
    b, t, d = x.shape
    tm = _row_block(t, block_cap)
    nb = t // tm
    n = b * t
    q = pl.pallas_call(
        _query_kernel,
        out_shape=jax.ShapeDtypeStruct((n, MLA_HEADS * QK_PAD), BF16),
        grid=(n // tm,),
        in_specs=[pl.BlockSpec((tm, d), lambda i: (i, 0)),
                  pl.BlockSpec((None, 1, 6 * d), lambda i: (i // nb, 0, 0)),
                  pl.BlockSpec((1, d), lambda i: (0, 0)),
                  pl.BlockSpec(w_dq.shape, lambda i: (0, 0)),
                  pl.BlockSpec((1, Q_LORA), lambda i: (0, 0)),
                  pl.BlockSpec(w_q.shape, lambda i: (0, 0)),
                  pl.BlockSpec(w_qs.shape, lambda i: (0, 0)),
                  pl.BlockSpec((tm, QK_PAD), lambda i: (i % nb, 0)),
                  pl.BlockSpec((tm, QK_PAD), lambda i: (i % nb, 0))],
        out_specs=pl.BlockSpec((tm, MLA_HEADS * QK_PAD), lambda i: (i, 0)),
        compiler_params=_cparams("parallel"),
        name="mla_queries",
    )(x.reshape(n, d), mods, norm1_g.reshape(1, d), w_dq, q_norm_g.reshape(1, Q_LORA), w_q, w_qs, cos_q, sin_q)
    return q.reshape(b, t, MLA_HEADS * QK_PAD)


def _kv_kernel(c_ref, kr_ref, wuk_ref, wuv_ref, k_ref, v_ref):
    cb = c_ref[...].astype(BF16)
    kn = _dot(cb, wuk_ref[...])
    vv = _dot(cb, wuv_ref[...])
    tm = c_ref.shape[0]
    tail = jnp.concatenate([kr_ref[...], jnp.zeros((tm, QK_PAD - QK_NOPE - QK_ROPE), F32)], axis=1)
    tail = tail.astype(k_ref.dtype)
    ones_col = (lax.broadcasted_iota(jnp.int32, (tm, V_PAD - V_HEAD), 1) == 0).astype(v_ref.dtype)
    for h in range(MLA_HEADS):
        k_ref[:, h * QK_PAD:h * QK_PAD + QK_NOPE] = kn[:, h * QK_NOPE:(h + 1) * QK_NOPE].astype(k_ref.dtype)
        k_ref[:, h * QK_PAD + QK_NOPE:(h + 1) * QK_PAD] = tail
        v_ref[:, h * V_PAD:h * V_PAD + V_HEAD] = vv[:, h * V_HEAD:(h + 1) * V_HEAD].astype(v_ref.dtype)
        v_ref[:, h * V_PAD + V_HEAD:(h + 1) * V_PAD] = ones_col


def _mla_keys_values(c_all, kr_all, w_uk, w_uv, *, block_cap=512):
    b, tk, _ = c_all.shape
    n = b * tk
    tm = _row_block(n, block_cap)
    k, v = pl.pallas_call(
        _kv_kernel,
        out_shape=[jax.ShapeDtypeStruct((n, MLA_HEADS * QK_PAD), BF16),
                   jax.ShapeDtypeStruct((n, MLA_HEADS * V_PAD), BF16)],
        grid=(n // tm,),
        in_specs=[pl.BlockSpec((tm, KV_LORA), lambda i: (i, 0)),
                  pl.BlockSpec((tm, QK_ROPE), lambda i: (i, 0)),
                  pl.BlockSpec(w_uk.shape, lambda i: (0, 0)),
                  pl.BlockSpec(w_uv.shape, lambda i: (0, 0))],
        out_specs=[pl.BlockSpec((tm, MLA_HEADS * QK_PAD), lambda i: (i, 0)),
                   pl.BlockSpec((tm, MLA_HEADS * V_PAD), lambda i: (i, 0))],
        compiler_params=_cparams("parallel"),
        name="mla_keys_values",
    )(c_all.reshape(n, KV_LORA), kr_all.reshape(n, QK_ROPE), w_uk, w_uv)
    return k.reshape(b, tk, -1), v.reshape(b, tk, -1)


def _attn_kernel(q_ref, k_ref, v_ref, o_ref, *, tk, pos0, n_keys):
    tq = q_ref.shape[0]
    q = q_ref[...]
    q_pos0 = pos0 + pl.program_id(2) * tq
    q_chunk = (q_pos0 + lax.broadcasted_iota(jnp.int32, (tq, 1), 0)) // CHUNK
    k_end = jnp.minimum(((q_pos0 + tq - 1) // CHUNK + 1) * CHUNK, n_keys)
    n_blocks = (k_end + tk - 1) // tk
    n_full = jnp.minimum((q_pos0 // CHUNK + 1) * CHUNK, n_keys) // tk
    c_exp = SM_SCALE * math.log2(math.e)

    n_split = 4 if tk % (4 * 2 * LANES) == 0 else 1
    th = tk // n_split

    def block(kb, carry, masked):
        m, acc = carry
        starts = [pl.multiple_of(kb * tk + i * th, th) for i in range(n_split)]
        scores = [_dot_nt(q, k_ref[pl.ds(st, th), :]) for st in starts]
        for st, s in zip(starts, scores):
            if masked:
                k_pos = st + lax.broadcasted_iota(jnp.int32, (1, th), 1)
                visible = (k_pos // CHUNK <= q_chunk) & (k_pos < n_keys)
                s = jnp.where(visible, s, -jnp.inf)
            m_new = jnp.maximum(m, jnp.max(s, axis=-1, keepdims=True))
            p = jnp.exp2((s - m_new) * c_exp)
            alpha = jnp.exp2((m - m_new) * c_exp)
            acc = alpha * acc + _dot(p.astype(BF16), v_ref[pl.ds(st, th), :])
            m = m_new
        return m, acc

    init = (jnp.full((tq, 1), -jnp.inf, F32), jnp.zeros((tq, V_PAD), F32))
    carry = lax.fori_loop(0, n_full, functools.partial(block, masked=False), init)
    _, acc = lax.fori_loop(n_full, n_blocks, functools.partial(block, masked=True), carry)
    o_ref[...] = (acc[:, :V_HEAD] / acc[:, V_HEAD:V_HEAD + 1]).astype(o_ref.dtype)


def _attention(q, k, v, *, pos0, n_keys):
    b, t, _ = q.shape
    tkeys = k.shape[1]
    tq = _row_block(t, 512)
    tk = tkeys if tkeys <= ATTN_KEY_BLOCK * 2 + LANES else _row_block(tkeys, ATTN_KEY_BLOCK)
    return pl.pallas_call(
        functools.partial(_attn_kernel, tk=tk, pos0=pos0, n_keys=n_keys),
        out_shape=jax.ShapeDtypeStruct((b, t, MLA_HEADS * V_HEAD), BF16),
        grid=(b, MLA_HEADS, t // tq),
        in_specs=[pl.BlockSpec((None, tq, QK_PAD), lambda bi, h, i: (bi, i, h)),
                  pl.BlockSpec((None, tkeys, QK_PAD), lambda bi, h, i: (bi, 0, h)),
                  pl.BlockSpec((None, tkeys, V_PAD), lambda bi, h, i: (bi, 0, h))],
        out_specs=pl.BlockSpec((None, tq, V_HEAD), lambda bi, h, i: (bi, i, h)),
        compiler_params=_cparams("parallel", "parallel", "arbitrary"),
        name="mla_attention",
    )(q, k, v)


def _rope_tables(pos):
    half = QK_ROPE // 2
    inv = ROPE_THETA ** (-jnp.arange(half, dtype=F32) / half)
    ang = pos.astype(F32)[:, None] * inv[None, :]
    cos, sin = jnp.cos(ang), jnp.sin(ang)
    t = pos.shape[0]
    cs_k = jnp.concatenate([cos, cos, -sin, sin], axis=1)
    one = jnp.ones((t, QK_NOPE), F32)
    zero_n = jnp.zeros((t, QK_NOPE), F32)
    zero_p = jnp.zeros((t, QK_PAD - QK_NOPE - QK_ROPE), F32)
    cos_q = jnp.concatenate([one, cos, cos, zero_p], axis=1)
    sin_q = jnp.concatenate([zero_n, -sin, sin, zero_p], axis=1)
    return cs_k, cos_q, sin_q


def _swap_halves(w):
    half = QK_ROPE // 2
    return jnp.concatenate([w[..., half:], w[..., :half]], axis=-1)


def _prep_weights(p):
    o1 = CONV_DIM + GDN_HEADS * GDN_DV
    w = {}
    w_in = p['gdn_w_in']
    w['gdn_qkv'] = w_in[:, :, :CONV_DIM].astype(BF16)
    w['gdn_gate'] = w_in[:, :, CONV_DIM:o1].astype(BF16)
    w['gdn_ab'] = jnp.pad(w_in[:, :, o1:], ((0, 0), (0, 0), (0, LANES - 2 * GDN_HEADS))).astype(BF16)
    w['gdn_o'] = p['gdn_w_o'].astype(BF16)
    w_dkv = p['mla_w_dkv']
    w['dkv_c'] = w_dkv[:, :KV_LORA].astype(BF16)
    rope_cols = w_dkv[:, KV_LORA:]
    w['dkv_r2'] = jnp.concatenate([rope_cols, _swap_halves(rope_cols)], axis=1).astype(BF16)
    w['uk'] = p['mla_w_uk'].astype(BF16)
    w['uv'] = p['mla_w_uv'].astype(BF16)
    w['dq'] = p['mla_w_dq'].astype(BF16)
    n_b = p['mla_w_uq'].shape[0]
    uq = p['mla_w_uq'].reshape(n_b, Q_LORA, MLA_HEADS, QK_NOPE + QK_ROPE)
    pad = jnp.zeros((n_b, Q_LORA, MLA_HEADS, QK_PAD - QK_NOPE - QK_ROPE), F32)
    w['uq'] = jnp.concatenate([uq, pad], axis=-1).reshape(n_b, Q_LORA, MLA_HEADS * QK_PAD).astype(BF16)
    uq_s = jnp.concatenate([jnp.zeros_like(uq[..., :QK_NOPE]), _swap_halves(uq[..., QK_NOPE:]), pad], axis=-1)
    w['uq_swap'] = uq_s.reshape(n_b, Q_LORA, MLA_HEADS * QK_PAD).astype(BF16)
    w['mla_o'] = p['mla_w_o'].astype(BF16)
    w['peer_q'] = p['peer_w_q'].astype(BF16)
    w['peer_keys'] = p['peer_sub_keys'].astype(BF16)
    w['peer_u'] = p['peer_u'].astype(BF16)
    pv = p['peer_v'].astype(BF16)
    pv = pv.reshape(pv.shape[0], N_EXPERTS // PEER_EXPERT_BLOCK, PEER_EXPERT_BLOCK, pv.shape[-1])
    w['peer_vt'] = jnp.swapaxes(pv, 2, 3)
    return w


def _trunk(x, mods, pos0, conv_bufs, delta_states, past_c, past_kr, p, w):
    b, t, d = x.shape
    depth = mods.shape[0]
    n_a = p['gdn_w_in'].shape[0]
    new_states, new_bufs = [], []
    c_kv = k_r = k_cat = v_all = None
    n_keys = past_c.shape[1] + t
    for l in range(depth):
        m = mods[l]
        if l < n_a:
            qkv, gate, ab = _norm_matmul(x, m, p['norm1_g'][l], [w['gdn_qkv'][l], w['gdn_gate'][l], w['gdn_ab'][l]],
                                         [F32, F32, F32], mod_off=0)
            o, s = _gdn(qkv, gate, ab, conv_bufs[l], delta_states[l], p['gdn_conv_w'][l], p['gdn_a_log'][l],
                        p['gdn_dt_bias'][l], p['gdn_onorm_g'][l])
            new_bufs.append(qkv[:, t - (CONV_W - 1):, :])
            new_states.append(s)
            w_o = w['gdn_o'][l]
        else:
            j = l - n_a
            cs_k, cos_q, sin_q = _rope_tables(pos0 + jnp.arange(t))
            if j == 0:
                c_kv, k_r = _shared_latent(x, p['kv_src_g'], w['dkv_c'], w['dkv_r2'], cs_k, p['mla_kv_norm_g'])
                c_all = jnp.concatenate([past_c, c_kv], axis=1)
                kr_all = jnp.concatenate([past_kr, k_r], axis=1)
                pad = -n_keys % LANES
                c_all = jnp.pad(c_all, ((0, 0), (0, pad), (0, 0)))
                kr_all = jnp.pad(kr_all, ((0, 0), (0, pad), (0, 0)))
                k_cat, v_all = _mla_keys_values(c_all, kr_all, w['uk'], w['uv'])
            q = _mla_queries(x, m, p['norm1_g'][l], w['dq'][j], p['mla_q_norm_g'][j], w['uq'][j], w['uq_swap'][j],
                             cos_q, sin_q)
            o = _attention(q, k_cat, v_all, pos0=pos0, n_keys=n_keys)
            w_o = w['mla_o'][j]
        x, h2 = _out_residual(o, w_o, x, m, p['norm2_g'][l])
        x = _peer(h2, x, m, w['peer_q'][l], w['peer_keys'][l], w['peer_u'][l], w['peer_vt'][l], p['final_g'],
                  final_norm=(l == depth - 1))
    return x, jnp.stack(new_states), jnp.stack(new_bufs), c_kv, k_r


def kernel(x_prompt, x_sample, c_prompt, c_sample, state_delta, state_conv, cache_kv_latent, cache_k_rope, w_mod, b_mod, norm1_g, norm2_g, final_g, gdn_w_in, gdn_conv_w, gdn_a_log, gdn_dt_bias, gdn_onorm_g, gdn_w_o, kv_src_g, mla_w_dkv, mla_kv_norm_g, mla_w_uk, mla_w_uv, mla_w_dq, mla_q_norm_g, mla_w_uq, mla_w_o, peer_w_q, peer_sub_keys, peer_u, peer_v):
    p = dict(norm1_g=norm1_g, norm2_g=norm2_g, final_g=final_g, gdn_w_in=gdn_w_in, gdn_conv_w=gdn_conv_w,
             gdn_a_log=gdn_a_log, gdn_dt_bias=gdn_dt_bias, gdn_onorm_g=gdn_onorm_g, gdn_w_o=gdn_w_o,
             kv_src_g=kv_src_g, mla_w_dkv=mla_w_dkv, mla_kv_norm_g=mla_kv_norm_g, mla_w_uk=mla_w_uk,
             mla_w_uv=mla_w_uv, mla_w_dq=mla_w_dq, mla_q_norm_g=mla_q_norm_g, mla_w_uq=mla_w_uq, mla_w_o=mla_w_o,
             peer_w_q=peer_w_q, peer_sub_keys=peer_sub_keys, peer_u=peer_u, peer_v=peer_v)
    w = _prep_weights(p)
    bp, tp, d = x_prompt.shape
    bs, ts, _ = x_sample.shape
    n_a = gdn_w_in.shape[0]

    rows = -(-(bp + bs) // SUBLANES) * SUBLANES
    c_all = jnp.pad(jnp.concatenate([c_prompt, c_sample], axis=0), ((0, rows - bp - bs), (0, 0)))
    mods = _modulation(c_all, w_mod, b_mod)
    mods_p = mods[:, :bp, None, :]
    mods_s = mods[:, bp:bp + bs, None, :]

    zero_bufs = jnp.zeros((n_a, bp, CONV_W - 1, CONV_DIM), F32)
    zero_states = jnp.zeros((n_a, bp, GDN_HEADS, GDN_DK, GDN_DV), F32)
    no_c = jnp.zeros((bp, 0, KV_LORA), F32)
    no_kr = jnp.zeros((bp, 0, QK_ROPE), F32)
    y_p, p_delta, p_conv, p_lat, p_kr = _trunk(x_prompt, mods_p, 0, zero_bufs, zero_states, no_c, no_kr, p, w)
    past = cache_kv_latent.shape[1]
    y_s, s_delta, s_conv, s_lat, s_kr = _trunk(x_sample, mods_s, past, state_conv, state_delta, cache_kv_latent,
                                               cache_k_rope, p, w)
    return (y_p, y_s, p_delta, p_conv, p_lat, p_kr, s_delta, s_conv, s_lat, s_kr)
```

```python
import functools
import math

import jax
import jax.numpy as jnp
from jax import lax
from jax.experimental import pallas as pl
from jax.experimental.pallas import tpu as pltpu

F32 = jnp.float32
BF16 = jnp.bfloat16
HIGHEST = lax.Precision.HIGHEST

D_MODEL = 1024
EPS = 1e-6
CHUNK = 64

GDN_HEADS = 8
GDN_DK = 128
GDN_DV = 128
CONV_W = 4
GDN_QK = GDN_HEADS * GDN_DK
CONV_DIM = GDN_HEADS * (2 * GDN_DK + GDN_DV)

MLA_HEADS = 8
QK_NOPE = 128
QK_ROPE = 64
V_HEAD = 128
KV_LORA = 256
Q_LORA = 384
ROPE_THETA = 10000.0
SM_SCALE = (QK_NOPE + QK_ROPE) ** -0.5
QK_PAD = 256
V_PAD = 256
ATTN_KEY_BLOCK = 1024

N_KEYS = 128
N_EXPERTS = N_KEYS * N_KEYS
PEER_HEADS = 8
PEER_TOPK = 16
PK_HALF = 128
PEER_EXPERT_BLOCK = 512

LANES = 128
SUBLANES = 8
BF16_ROWS = 16
VMEM_LIMIT = 56 * 1024 * 1024


def _cparams(*sem):
    return pltpu.CompilerParams(dimension_semantics=sem, vmem_limit_bytes=VMEM_LIMIT)


def _silu(x):
    return x * jax.nn.sigmoid(x)


def _softplus(x):
    return jnp.maximum(x, 0.0) + jnp.log1p(jnp.exp(-jnp.abs(x)))


def _rms(x):
    return x * lax.rsqrt(jnp.mean(x * x, axis=-1, keepdims=True) + EPS)


def _dot(a, b, **kw):
    return jnp.dot(a, b, preferred_element_type=F32, **kw)


def _dot_nt(a, b, **kw):
    return lax.dot_general(a, b, (((1,), (1,)), ((), ())), preferred_element_type=F32, **kw)


def _dot_tn(a, b, **kw):
    return lax.dot_general(a, b, (((0,), (0,)), ((), ())), preferred_element_type=F32, **kw)


def _split_bf16(a):
    hi = a.astype(BF16)
    return hi, (a - hi.astype(F32)).astype(BF16)


def _dot3(a, b):
    ah, al = _split_bf16(a)
    bh, bl = _split_bf16(b)
    return _dot(jnp.concatenate([ah, ah, al], axis=1), jnp.concatenate([bh, bl, bh], axis=0))


def _row_block(t, cap):
    b = min(t, cap)
    while t % b:
        b //= 2
    return b


def _mod_kernel(c_ref, w_ref, b_ref, o_ref):
    a = _silu(c_ref[...]).astype(BF16)
    o_ref[...] = _dot(a, w_ref[...].astype(BF16)) + b_ref[...]


def _modulation(c_pad, w_mod, b_mod):
    depth, d, n = w_mod.shape
    r = c_pad.shape[0]
    tn = 1024
    return pl.pallas_call(
        _mod_kernel,
        out_shape=jax.ShapeDtypeStruct((depth, r, n), F32),
        grid=(depth, n // tn),
        in_specs=[pl.BlockSpec((r, d), lambda l, j: (0, 0)),
                  pl.BlockSpec((None, d, tn), lambda l, j: (l, 0, j)),
                  pl.BlockSpec((None, 1, tn), lambda l, j: (l, 0, j))],
        out_specs=pl.BlockSpec((None, r, tn), lambda l, j: (l, 0, j)),
        compiler_params=_cparams("arbitrary", "arbitrary"),
        name="adaln_modulation",
    )(c_pad, w_mod, b_mod.reshape(depth, 1, n))


def _norm_mm_kernel(*refs, n_w, mod_off):
    x_ref, m_ref, g_ref = refs[:3]
    w_refs = refs[3:3 + n_w]
    o_refs = refs[3 + n_w:]
    y = _rms(x_ref[...]) * g_ref[...]
    if mod_off is not None:
        sh = m_ref[:, mod_off * D_MODEL:(mod_off + 1) * D_MODEL]
        sc = m_ref[:, (mod_off + 1) * D_MODEL:(mod_off + 2) * D_MODEL]
        y = y * (1.0 + sc) + sh
    yb = y.astype(BF16)
    for w_ref, o_ref in zip(w_refs, o_refs):
        o_ref[...] = _dot(yb, w_ref[...]).astype(o_ref.dtype)


def _norm_matmul(x, mods, g, weights, out_dtypes, *, mod_off, block_cap=512):
    b, t, d = x.shape
    tm = _row_block(t, block_cap)
    nb = t // tm
    n = b * t
    x2 = x.reshape(n, d)
    in_specs = [pl.BlockSpec((tm, d), lambda i: (i, 0)),
                pl.BlockSpec((None, 1, 6 * d), lambda i: (i // nb, 0, 0)),
                pl.BlockSpec((1, d), lambda i: (0, 0))]
    in_specs += [pl.BlockSpec(w.shape, lambda i: (0, 0)) for w in weights]
    outs = pl.pallas_call(
        functools.partial(_norm_mm_kernel, n_w=len(weights), mod_off=mod_off),
        out_shape=[jax.ShapeDtypeStruct((n, w.shape[1]), dt) for w, dt in zip(weights, out_dtypes)],
        grid=(n // tm,),
        in_specs=in_specs,
        out_specs=[pl.BlockSpec((tm, w.shape[1]), lambda i: (i, 0)) for w in weights],
        compiler_params=_cparams("parallel"),
        name="norm_matmul",
    )(x2, mods, g.reshape(1, d), *weights)
    return [o.reshape(b, t, o.shape[-1]) for o in outs]


def _gdn_kernel(qkv_ref, gate_ref, ab_ref, cbuf_ref, s0_ref, cw_ref, par_ref, og_ref,
                o_ref, s_ref, xp_ref, *, chunk):
    c = chunk
    step = pl.program_id(1)

    @pl.when(step == 0)
    def _():
        s_ref[...] = s0_ref[...]
        xp_ref[0:SUBLANES, :] = cbuf_ref[...]

    xp_ref[SUBLANES:SUBLANES + c, :] = qkv_ref[...]
    base = SUBLANES - (CONV_W - 1)
    conv = xp_ref[base:base + c, :] * cw_ref[0:1, :]
    for w in range(1, CONV_W):
        conv = conv + xp_ref[base + w:base + w + c, :] * cw_ref[w:w + 1, :]
    xp_ref[0:SUBLANES, :] = xp_ref[c:c + SUBLANES, :]
    act = _silu(conv)

    ab = ab_ref[...]
    g_all = -jnp.exp(par_ref[0:1, :]) * _softplus(ab + par_ref[1:2, :])
    beta_all = jax.nn.sigmoid(ab)

    row = lax.broadcasted_iota(jnp.int32, (c, c), 0)
    col = lax.broadcasted_iota(jnp.int32, (c, c), 1)
    causal = row >= col
    strict = row > col
    eye = (row == col).astype(F32)
    lower = causal.astype(F32)
    upper = (row <= col).astype(F32)

    d_col_all = _dot(lower, g_all, precision=HIGHEST)
    d_row_all = _dot_tn(g_all, upper, precision=HIGHEST)

    n_double = int(math.log2(c)) - 1
    keep_s = lax.broadcasted_iota(jnp.int32, (c, 2 * c), 1) < c
    heads = range(GDN_HEADS)
    q, k, v, beta, d_col, d_last, decay, e_col, kb, kq = ([None] * GDN_HEADS for _ in range(10))
    for h in heads:
        qh = act[:, h * GDN_DK:(h + 1) * GDN_DK]
        kh = act[:, GDN_QK + h * GDN_DK:GDN_QK + (h + 1) * GDN_DK]
        v[h] = act[:, 2 * GDN_QK + h * GDN_DV:2 * GDN_QK + (h + 1) * GDN_DV]
        q[h] = qh * lax.rsqrt(jnp.sum(qh * qh, axis=-1, keepdims=True) + EPS) * (GDN_DK ** -0.5)
        k[h] = kh * lax.rsqrt(jnp.sum(kh * kh, axis=-1, keepdims=True) + EPS)
        beta[h] = beta_all[:, GDN_HEADS + h:GDN_HEADS + h + 1]
        d_col[h] = d_col_all[:, h:h + 1]
        d_last[h] = d_col_all[c - 1:c, h:h + 1]
        decay[h] = jnp.exp(jnp.where(causal, d_col[h] - d_row_all[h:h + 1, :], -jnp.inf))
        e_col[h] = jnp.exp(d_col[h])
        kb[h] = k[h] * beta[h]
    for h in heads:
        kq[h] = _dot_nt(jnp.concatenate([kb[h], q[h]], axis=0).astype(BF16), k[h].astype(BF16))
    m = [-jnp.where(strict, kq[h][:c] * decay[h], 0.0) for h in heads]
    x = [jnp.concatenate([eye, m[h]], axis=1) for h in heads]
    for _ in range(n_double):
        for h in heads:
            x[h] = jnp.where(keep_s, x[h], 0.0) + _dot3(m[h], x[h])
            m[h] = x[h][:, c:]
    p = [x[h][:, :c] + _dot3(m[h], x[h][:, :c]) for h in heads]
    sol = [_dot3(p[h], jnp.concatenate([v[h] * beta[h], kb[h] * e_col[h]], axis=1)) for h in heads]
    s = [s_ref[h] for h in heads]
    ws = [_dot(jnp.concatenate([sol[h][:, GDN_DV:], q[h] * e_col[h]], axis=0).astype(BF16), s[h].astype(BF16))
          for h in heads]
    vnb = [(sol[h][:, :GDN_DV] - ws[h][:c]).astype(BF16) for h in heads]
    o = [ws[h][c:] + _dot((kq[h][c:] * decay[h]).astype(BF16), vnb[h]) for h in heads]
    for h in heads:
        k_dec = (k[h] * jnp.exp(d_last[h] - d_col[h])).astype(BF16)
        s_ref[h] = s[h] * jnp.exp(d_last[h]) + _dot_tn(k_dec, vnb[h])
    for h in heads:
        gate = gate_ref[:, h * GDN_DV:(h + 1) * GDN_DV]
        o_ref[:, h * GDN_DV:(h + 1) * GDN_DV] = (_rms(o[h]) * og_ref[...] * _silu(gate)).astype(o_ref.dtype)


def _gdn(qkv, gate, ab, conv_buf, s0, conv_w, a_log, dt_bias, onorm_g):
    b, t, _ = qkv.shape
    c = min(CHUNK, t)
    cbuf = jnp.pad(conv_buf, ((0, 0), (SUBLANES - (CONV_W - 1), 0), (0, 0)))
    cw = jnp.pad(conv_w, ((0, SUBLANES - CONV_W), (0, 0)))
    par = jnp.zeros((SUBLANES, LANES), F32)
    par = par.at[0, :GDN_HEADS].set(a_log).at[1, :GDN_HEADS].set(dt_bias)
    o, s = pl.pallas_call(
        functools.partial(_gdn_kernel, chunk=c),
        out_shape=[jax.ShapeDtypeStruct((b, t, GDN_HEADS * GDN_DV), BF16),
                   jax.ShapeDtypeStruct((b, GDN_HEADS, GDN_DK, GDN_DV), F32)],
        grid=(b, t // c),
        in_specs=[pl.BlockSpec((None, c, CONV_DIM), lambda i, j: (i, j, 0)),
                  pl.BlockSpec((None, c, GDN_HEADS * GDN_DV), lambda i, j: (i, j, 0)),
                  pl.BlockSpec((None, c, LANES), lambda i, j: (i, j, 0)),
                  pl.BlockSpec((None, SUBLANES, CONV_DIM), lambda i, j: (i, 0, 0)),
                  pl.BlockSpec((None, GDN_HEADS, GDN_DK, GDN_DV), lambda i, j: (i, 0, 0, 0)),
                  pl.BlockSpec((SUBLANES, CONV_DIM), lambda i, j: (0, 0)),
                  pl.BlockSpec((SUBLANES, LANES), lambda i, j: (0, 0)),
                  pl.BlockSpec((1, GDN_DV), lambda i, j: (0, 0))],
        out_specs=[pl.BlockSpec((None, c, GDN_HEADS * GDN_DV), lambda i, j: (i, j, 0)),
                   pl.BlockSpec((None, GDN_HEADS, GDN_DK, GDN_DV), lambda i, j: (i, 0, 0, 0))],
        scratch_shapes=[pltpu.VMEM((c + SUBLANES, CONV_DIM), F32)],
        compiler_params=_cparams("parallel", "arbitrary"),
        name="gated_delta",
    )(qkv, gate, ab, cbuf, s0, cw, par, onorm_g.reshape(1, GDN_DV))
    return o, s


def _out_res_kernel(o_ref, w_ref, x_ref, m_ref, g2_ref, y_ref, h2_ref):
    def mod(i):
        return m_ref[:, i * D_MODEL:(i + 1) * D_MODEL]

    y = x_ref[...] + mod(2) * _dot(o_ref[...], w_ref[...])
    y_ref[...] = y
    h2_ref[...] = (_rms(y) * g2_ref[...] * (1.0 + mod(4)) + mod(3)).astype(h2_ref.dtype)


def _out_residual(o, w, x, mods, norm2_g, *, block_cap=512):
    b, t, d = x.shape
    kdim = o.shape[-1]
    tm = _row_block(t, block_cap)
    nb = t // tm
    n = b * t
    y, h2 = pl.pallas_call(
        _out_res_kernel,
        out_shape=[jax.ShapeDtypeStruct((n, d), F32), jax.ShapeDtypeStruct((n, d), BF16)],
        grid=(n // tm,),
        in_specs=[pl.BlockSpec((tm, kdim), lambda i: (i, 0)),
                  pl.BlockSpec((kdim, d), lambda i: (0, 0)),
                  pl.BlockSpec((tm, d), lambda i: (i, 0)),
                  pl.BlockSpec((None, 1, 6 * d), lambda i: (i // nb, 0, 0)),
                  pl.BlockSpec((1, d), lambda i: (0, 0))],
        out_specs=[pl.BlockSpec((tm, d), lambda i: (i, 0)), pl.BlockSpec((tm, d), lambda i: (i, 0))],
        compiler_params=_cparams("parallel"),
        name="out_proj_residual",
    )(o.reshape(n, kdim), w, x.reshape(n, d), mods, norm2_g.reshape(1, d))
    return y.reshape(b, t, d), h2


_CAND = [(j1, j2) for j1 in range(PEER_TOPK) for j2 in range(PEER_TOPK) if (j1 + 1) * (j2 + 1) <= PEER_TOPK]
_CAND_PAD = -(-len(_CAND) // SUBLANES) * SUBLANES


def _top_values(s, n_top):
    vals = [jnp.max(s, axis=0, keepdims=True)]
    for _ in range(1, n_top):
        vals.append(jnp.max(jnp.where(s < vals[-1], s, -jnp.inf), axis=0, keepdims=True))
    return vals


def _sort_network(n):
    pairs = []
    p = 1
    while p < n:
        k = p
        while k >= 1:
            for j in range(k % p, n - k, 2 * k):
                for i in range(min(k, n - j - k)):
                    if (i + j) // (2 * p) == (i + j + k) // (2 * p):
                        pairs.append((i + j, i + j + k))
            k //= 2
        p *= 2
    return pairs


def _top_values_keys(s, n_top):
    groups = s.shape[0] // SUBLANES
    assert groups == n_top
    r = [s[g * SUBLANES:(g + 1) * SUBLANES, :] for g in range(groups)]
    for lo, hi in _sort_network(groups):
        r[lo], r[hi] = jnp.maximum(r[lo], r[hi]), jnp.minimum(r[lo], r[hi])
    vals = []
    for t in range(n_top):
        m = jnp.max(r[0], axis=0, keepdims=True)
        vals.append(m)
        if t + 1 < n_top:
            pop = r[0] == m
            for k in range(n_top - 1 - t):
                r[k] = jnp.where(pop, r[k + 1], r[k])
    return vals


def _rank_among(s, vals):
    rank = jnp.zeros(s.shape, F32)
    for v in vals:
        rank = rank + jnp.where(s < v, 1.0, 0.0)
    return rank


def _peer_route_kernel(h2_ref, wq_ref, keys_ref, rank2_ref, e2_ref, n1_ref, e1_ref, q_scr):
    q_scr[...] = _dot(h2_ref[...], wq_ref[...]).astype(BF16)
    tm = h2_ref.shape[0]

    def head(h, carry):
        off = pl.multiple_of(h * 2 * PK_HALF, 2 * PK_HALF)
        s1 = _dot_nt(keys_ref[h, 0], q_scr[:, pl.ds(off, PK_HALF)])
        s2 = _dot_nt(keys_ref[h, 1], q_scr[:, pl.ds(off + PK_HALF, PK_HALF)])
        v1 = _top_values_keys(s1, PEER_TOPK)
        v2 = _top_values_keys(s2, PEER_TOPK)
        rank2 = _rank_among(s2, v2)
        rows = [v1[j1] + v2[j2] for j1, j2 in _CAND]
        rows += [jnp.full((1, tm), -jnp.inf, F32)] * (_CAND_PAD - len(_CAND))
        cand = jnp.concatenate(rows, axis=0)
        tau = _top_values(cand, PEER_TOPK)[-1]
        sel = jnp.where(cand >= tau, 1.0, 0.0)
        z = jnp.sum(sel * jnp.exp(cand - cand[0:1, :]), axis=0, keepdims=True)
        n1 = jnp.zeros(s1.shape, F32)
        off_c = 0
        for j1 in range(PEER_TOPK):
            cnt = PEER_TOPK // (j1 + 1)
            n_j = jnp.sum(sel[off_c:off_c + cnt, :], axis=0, keepdims=True)
            n1 = jnp.where(s1 == v1[j1], n_j, n1)
            off_c += cnt
        rank2_ref[h] = rank2.astype(rank2_ref.dtype)
        e2_ref[h] = jnp.exp(s2 - v2[0]).astype(e2_ref.dtype)
        n1_ref[h] = n1
        e1_ref[h] = jnp.exp(s1 - v1[0]) * (1.0 / z)
        return carry

    lax.fori_loop(0, PEER_HEADS, head, 0)


def _peer_route(h2, w_q, sub_keys, *, tm):
    n, d = h2.shape
    shp = (PEER_HEADS, N_KEYS, n)
    blk = pl.BlockSpec((PEER_HEADS, N_KEYS, tm), lambda i: (0, 0, i))
    return pl.pallas_call(
        _peer_route_kernel,
        out_shape=[jax.ShapeDtypeStruct(shp, BF16), jax.ShapeDtypeStruct(shp, BF16),
                   jax.ShapeDtypeStruct(shp, F32), jax.ShapeDtypeStruct(shp, F32)],
        grid=(n // tm,),
        in_specs=[pl.BlockSpec((tm, d), lambda i: (i, 0)),
                  pl.BlockSpec(w_q.shape, lambda i: (0, 0)),
                  pl.BlockSpec(sub_keys.shape, lambda i: (0, 0, 0, 0))],
        out_specs=[blk, blk, blk, blk],
        scratch_shapes=[pltpu.VMEM((tm, w_q.shape[1]), BF16)],
        compiler_params=_cparams("parallel"),
        name="peer_route",
    )(h2, w_q, sub_keys)


def _peer_dense_kernel(h2_ref, u_ref, vt_prev_ref, vt_cur_ref, rank2_ref, e2_ref, n1_ref, e1_ref, x_ref, m_ref,
                       fg_ref, y_ref, acc_a_ref, acc_b_ref, wa_a_ref, wa_b_ref, gate_a_ref, gate_b_ref, nb_ref, eb_ref,
                       *, eb, final_norm):
    j = pl.program_id(1)
    n_pairs = pl.num_programs(1) - 1
    rows_per_half = eb // N_KEYS
    tm = h2_ref.shape[0]
    zero = jnp.zeros((), BF16)

    @pl.when(j == 0)
    def _():
        acc_a_ref[...] = jnp.zeros_like(acc_a_ref)
        acc_b_ref[...] = jnp.zeros_like(acc_b_ref)
        wa_b_ref[...] = jnp.zeros_like(wa_b_ref)

    tn = tm // 2
    s_tiles = list(range(0, N_KEYS, BF16_ROWS))

    def stage_rows(half):
        for h in range(PEER_HEADS):
            n_all = n1_ref[h].astype(BF16)
            e_all = e1_ref[h].astype(BF16)
            for r in range(rows_per_half):
                k = (half * PEER_HEADS + h) * rows_per_half + r
                row = half * rows_per_half + r
                nb_ref[k] = jnp.broadcast_to(n_all[row:row + 1, :], (BF16_ROWS, tm))
                eb_ref[k] = jnp.broadcast_to(e_all[row:row + 1, :], (BF16_ROWS, tm))

    def gate_tiles(half, gate_ref, tiles):
        for s0 in tiles:
            w = [None] * rows_per_half
            for h in range(PEER_HEADS):
                r2 = rank2_ref[h, s0:s0 + BF16_ROWS, :]
                e2 = e2_ref[h, s0:s0 + BF16_ROWS, :]
                for r in range(rows_per_half):
                    k = (half * PEER_HEADS + h) * rows_per_half + r
                    term = jnp.where(r2 < nb_ref[k], e2, zero) * eb_ref[k]
                    w[r] = term if w[r] is None else w[r] + term
            for r in range(rows_per_half):
                gate_ref[r * N_KEYS + s0:r * N_KEYS + s0 + BF16_ROWS, :] = w[r]

    def pre_act(half, piece, wa_ref):
        lanes = slice(piece * tn, (piece + 1) * tn)
        wa_ref[:, lanes] = _dot_nt(u_ref[half * eb:(half + 1) * eb, :], h2_ref[lanes, :]).astype(BF16)

    def activate(piece, gate_ref, wa_ref):
        lanes = slice(piece * tn, (piece + 1) * tn)
        wa_ref[:, lanes] = jax.nn.gelu(wa_ref[:, lanes]) * gate_ref[:, lanes]

    def value_matmul(vt_ref, wa_ref, acc_ref, piece):
        lanes = slice(piece * tn, (piece + 1) * tn)
        acc_ref[:, lanes] += _dot(vt_ref[...], wa_ref[:, lanes])

    @pl.when(j < n_pairs)
    def _():
        stage_rows(0)
        stage_rows(1)
        pre_act(0, 0, wa_a_ref)
        gate_tiles(0, gate_a_ref, s_tiles[:4])
        pre_act(0, 1, wa_a_ref)
        gate_tiles(0, gate_a_ref, s_tiles[4:])
        value_matmul(vt_prev_ref, wa_b_ref, acc_b_ref, 0)
        activate(0, gate_a_ref, wa_a_ref)
        value_matmul(vt_prev_ref, wa_b_ref, acc_b_ref, 1)
        activate(1, gate_a_ref, wa_a_ref)
        pre_act(1, 0, wa_b_ref)
        gate_tiles(1, gate_b_ref, s_tiles[:4])
        pre_act(1, 1, wa_b_ref)
        gate_tiles(1, gate_b_ref, s_tiles[4:])
        value_matmul(vt_cur_ref, wa_a_ref, acc_a_ref, 0)
        activate(0, gate_b_ref, wa_b_ref)
        value_matmul(vt_cur_ref, wa_a_ref, acc_a_ref, 1)
        activate(1, gate_b_ref, wa_b_ref)

    @pl.when(j == n_pairs)
    def _():
        nseq, _, _ = m_ref.shape
        d = x_ref.shape[1]
        g2 = m_ref[:, :, 5 * D_MODEL:6 * D_MODEL]
        acc = acc_a_ref[...] + acc_b_ref[...] + _dot(vt_prev_ref[...], wa_b_ref[...])
        out = acc.T.reshape(nseq, tm // nseq, d)
        y = (x_ref[...].reshape(nseq, tm // nseq, d) + g2 * out).reshape(tm, d)
        if final_norm:
            y = _rms(y) * fg_ref[...]
        y_ref[...] = y


def _peer_dense(h2, route, u_b, vt_b, x, mods, final_g, *, tm, eb, final_norm):
    b, t, d = x.shape
    n = b * t
    nseq = max(1, tm // t)
    rank2, e2, n1, e1 = route
    tok = pl.BlockSpec((PEER_HEADS, N_KEYS, tm), lambda i, j: (0, 0, i))
    n_pairs = N_EXPERTS // (2 * eb)
    n_bcast = 2 * PEER_HEADS * (eb // N_KEYS)
    key_rows = pl.BlockSpec((PEER_HEADS, 2 * eb // N_KEYS, tm), lambda i, j: (0, jnp.minimum(j, n_pairs - 1), i))
    y = pl.pallas_call(
        functools.partial(_peer_dense_kernel, eb=eb, final_norm=final_norm),
        out_shape=jax.ShapeDtypeStruct((n, d), F32),
        grid=(n // tm, n_pairs + 1),
        in_specs=[pl.BlockSpec((tm, d), lambda i, j: (i, 0)),
                  pl.BlockSpec((2 * eb, d), lambda i, j: (jnp.minimum(j, n_pairs - 1), 0)),
                  pl.BlockSpec((None, d, eb), lambda i, j: (jnp.maximum(2 * j - 1, 0), 0, 0)),
                  pl.BlockSpec((None, d, eb), lambda i, j: (jnp.minimum(2 * j, 2 * n_pairs - 2), 0, 0)),
                  tok, tok, key_rows, key_rows,
                  pl.BlockSpec((tm, d), lambda i, j: (i, 0)),
                  pl.BlockSpec((nseq, 1, 6 * d), lambda i, j: ((i * tm) // t // nseq, 0, 0)),
                  pl.BlockSpec((1, d), lambda i, j: (0, 0))],
        out_specs=pl.BlockSpec((tm, d), lambda i, j: (i, 0)),
        scratch_shapes=[pltpu.VMEM((d, tm), F32), pltpu.VMEM((d, tm), F32),
                        pltpu.VMEM((eb, tm), BF16), pltpu.VMEM((eb, tm), BF16),
                        pltpu.VMEM((eb, tm), BF16), pltpu.VMEM((eb, tm), BF16),
                        pltpu.VMEM((n_bcast, BF16_ROWS, tm), BF16), pltpu.VMEM((n_bcast, BF16_ROWS, tm), BF16)],
        compiler_params=_cparams("parallel", "arbitrary"),
        name="peer_experts",
    )(h2, u_b, vt_b, vt_b, rank2, e2, n1, e1, x.reshape(n, d), mods, final_g.reshape(1, d))
    return y.reshape(b, t, d)


def _peer(h2, x, mods, w_q, sub_keys, u_b, vt_b, final_g, *, final_norm):
    n = h2.shape[0]
    tm = _row_block(n, 512)
    route = _peer_route(h2, w_q, sub_keys, tm=min(tm, 256))
    return _peer_dense(h2, route, u_b, vt_b, x, mods, final_g, tm=tm, eb=PEER_EXPERT_BLOCK, final_norm=final_norm)


def _latent_kernel(x_ref, g_ref, wc_ref, wr_ref, cs_ref, kvg_ref, c_ref, kr_ref):
    xn = (_rms(x_ref[...]) * g_ref[...]).astype(BF16)
    c_ref[...] = _rms(_dot(xn, wc_ref[...])) * kvg_ref[...]
    p = _dot(xn, wr_ref[...]) * cs_ref[...]
    kr_ref[...] = p[:, :QK_ROPE] + p[:, QK_ROPE:]


def _shared_latent(x, kv_src_g, w_c, w_r2, cs_k, kv_norm_g, *, block_cap=512):
    b, t, d = x.shape
    tm = _row_block(t, block_cap)
    nb = t // tm
    n = b * t
    c_kv, k_r = pl.pallas_call(
        _latent_kernel,
        out_shape=[jax.ShapeDtypeStruct((n, KV_LORA), F32), jax.ShapeDtypeStruct((n, QK_ROPE), F32)],
        grid=(n // tm,),
        in_specs=[pl.BlockSpec((tm, d), lambda i: (i, 0)),
                  pl.BlockSpec((1, d), lambda i: (0, 0)),
                  pl.BlockSpec(w_c.shape, lambda i: (0, 0)),
                  pl.BlockSpec(w_r2.shape, lambda i: (0, 0)),
                  pl.BlockSpec((tm, 2 * QK_ROPE), lambda i: (i % nb, 0)),
                  pl.BlockSpec((1, KV_LORA), lambda i: (0, 0))],
        out_specs=[pl.BlockSpec((tm, KV_LORA), lambda i: (i, 0)), pl.BlockSpec((tm, QK_ROPE), lambda i: (i, 0))],
        compiler_params=_cparams("parallel"),
        name="mla_shared_latent",
    )(x.reshape(n, d), kv_src_g.reshape(1, d), w_c, w_r2, cs_k, kv_norm_g.reshape(1, KV_LORA))
    return c_kv.reshape(b, t, KV_LORA), k_r.reshape(b, t, QK_ROPE)


def _query_kernel(x_ref, m_ref, g_ref, wdq_ref, qg_ref, wq_ref, wqs_ref, c_ref, s_ref, q_ref):
    y = _rms(x_ref[...]) * g_ref[...]
    y = y * (1.0 + m_ref[:, D_MODEL:2 * D_MODEL]) + m_ref[:, 0:D_MODEL]
    cq = (_rms(_dot(y.astype(BF16), wdq_ref[...])) * qg_ref[...]).astype(BF16)
    q = _dot(cq, wq_ref[...])
    qs = _dot(cq, wqs_ref[...])
    cos = c_ref[...]
    sin = s_ref[...]
    for h in range(MLA_HEADS):
        sl = slice(h * QK_PAD, (h + 1) * QK_PAD)
        q_ref[:, sl] = (q[:, sl] * cos + qs[:, sl] * sin).astype(q_ref.dtype)


def _mla_queries(x, mods, norm1_g, w_dq, q_norm_g, w_q, w_qs, cos_q, sin_q, *, block_cap=512):
    b, t, d = x.shape
    tm = _row_block(t, block_cap)
    nb = t // tm
    n = b * t
    q = pl.pallas_call(
        _query_kernel,
        out_shape=jax.ShapeDtypeStruct((n, MLA_HEADS * QK_PAD), BF16),
        grid=(n // tm,),
        in_specs=[pl.BlockSpec((tm, d), lambda i: (i, 0)),
                  pl.BlockSpec((None, 1, 6 * d), lambda i: (i // nb, 0, 0)),
                  pl.BlockSpec((1, d), lambda i: (0, 0)),
                  pl.BlockSpec(w_dq.shape, lambda i: (0, 0)),
                  pl.BlockSpec((1, Q_LORA), lambda i: (0, 0)),
                  pl.BlockSpec(w_q.shape, lambda i: (0, 0)),
                  pl.BlockSpec(w_qs.shape, lambda i: (0, 0)),
                  pl.BlockSpec((tm, QK_PAD), lambda i: (i % nb, 0)),
                  pl.BlockSpec((tm, QK_PAD), lambda i: (i % nb, 0))],
        out_specs=pl.BlockSpec((tm, MLA_HEADS * QK_PAD), lambda i: (i, 0)),
        compiler_params=_cparams("parallel"),
        name="mla_queries",
    )(x.reshape(n, d), mods, norm1_g.reshape(1, d), w_dq, q_norm_g.reshape(1, Q_LORA), w_q, w_qs, cos_q, sin_q)
    return q.reshape(b, t, MLA_HEADS * QK_PAD)


def _kv_kernel(c_ref, kr_ref, wuk_ref, wuv_ref, k_ref, v_ref):
    cb = c_ref[...].astype(BF16)
    kn = _dot(cb, wuk_ref[...])
    vv = _dot(cb, wuv_ref[...])
    tm = c_ref.shape[0]
    tail = jnp.concatenate([kr_ref[...], jnp.zeros((tm, QK_PAD - QK_NOPE - QK_ROPE), F32)], axis=1)
    tail = tail.astype(k_ref.dtype)
    ones_col = (lax.broadcasted_iota(jnp.int32, (tm, V_PAD - V_HEAD), 1) == 0).astype(v_ref.dtype)
    for h in range(MLA_HEADS):
        k_ref[:, h * QK_PAD:h * QK_PAD + QK_NOPE] = kn[:, h * QK_NOPE:(h + 1) * QK_NOPE].astype(k_ref.dtype)
        k_ref[:, h * QK_PAD + QK_NOPE:(h + 1) * QK_PAD] = tail
        v_ref[:, h * V_PAD:h * V_PAD + V_HEAD] = vv[:, h * V_HEAD:(h + 1) * V_HEAD].astype(v_ref.dtype)
        v_ref[:, h * V_PAD + V_HEAD:(h + 1) * V_PAD] = ones_col


def _mla_keys_values(c_all, kr_all, w_uk, w_uv, *, block_cap=512):
    b, tk, _ = c_all.shape
    n = b * tk
    tm = _row_block(n, block_cap)
    k, v = pl.pallas_call(
        _kv_kernel,
        out_shape=[jax.ShapeDtypeStruct((n, MLA_HEADS * QK_PAD), BF16),
                   jax.ShapeDtypeStruct((n, MLA_HEADS * V_PAD), BF16)],
        grid=(n // tm,),
        in_specs=[pl.BlockSpec((tm, KV_LORA), lambda i: (i, 0)),
                  pl.BlockSpec((tm, QK_ROPE), lambda i: (i, 0)),
                  pl.BlockSpec(w_uk.shape, lambda i: (0, 0)),
                  pl.BlockSpec(w_uv.shape, lambda i: (0, 0))],
        out_specs=[pl.BlockSpec((tm, MLA_HEADS * QK_PAD), lambda i: (i, 0)),
                   pl.BlockSpec((tm, MLA_HEADS * V_PAD), lambda i: (i, 0))],
        compiler_params=_cparams("parallel"),
        name="mla_keys_values",
    )(c_all.reshape(n, KV_LORA), kr_all.reshape(n, QK_ROPE), w_uk, w_uv)
    return k.reshape(b, tk, -1), v.reshape(b, tk, -1)


def _attn_kernel(q_ref, k_ref, v_ref, o_ref, *, tk, pos0, n_keys):
    tq = q_ref.shape[0]
    q = q_ref[...]
    q_pos0 = pos0 + pl.program_id(2) * tq
    q_chunk = (q_pos0 + lax.broadcasted_iota(jnp.int32, (tq, 1), 0)) // CHUNK
    k_end = jnp.minimum(((q_pos0 + tq - 1) // CHUNK + 1) * CHUNK, n_keys)
    n_blocks = (k_end + tk - 1) // tk
    n_full = jnp.minimum((q_pos0 // CHUNK + 1) * CHUNK, n_keys) // tk
    c_exp = SM_SCALE * math.log2(math.e)

    n_split = 4 if tk % (4 * 2 * LANES) == 0 else 1
    th = tk // n_split

    def block(kb, carry, masked):
        m, acc = carry
        starts = [pl.multiple_of(kb * tk + i * th, th) for i in range(n_split)]
        scores = [_dot_nt(q, k_ref[pl.ds(st, th), :]) for st in starts]
        for st, s in zip(starts, scores):
            if masked:
                k_pos = st + lax.broadcasted_iota(jnp.int32, (1, th), 1)
                visible = (k_pos // CHUNK <= q_chunk) & (k_pos < n_keys)
                s = jnp.where(visible, s, -jnp.inf)
            m_new = jnp.maximum(m, jnp.max(s, axis=-1, keepdims=True))
            p = jnp.exp2((s - m_new) * c_exp)
            alpha = jnp.exp2((m - m_new) * c_exp)
            acc = alpha * acc + _dot(p.astype(BF16), v_ref[pl.ds(st, th), :])
            m = m_new
        return m, acc

    init = (jnp.full((tq, 1), -jnp.inf, F32), jnp.zeros((tq, V_PAD), F32))
    carry = lax.fori_loop(0, n_full, functools.partial(block, masked=False), init)
    _, acc = lax.fori_loop(n_full, n_blocks, functools.partial(block, masked=True), carry)
    o_ref[...] = (acc[:, :V_HEAD] / acc[:, V_HEAD:V_HEAD + 1]).astype(o_ref.dtype)


def _attention(q, k, v, *, pos0, n_keys):
    b, t, _ = q.shape
    tkeys = k.shape[1]
    tq = _row_block(t, 512)
    tk = tkeys if tkeys <= ATTN_KEY_BLOCK * 2 + LANES else _row_block(tkeys, ATTN_KEY_BLOCK)
    return pl.pallas_call(
        functools.partial(_attn_kernel, tk=tk, pos0=pos0, n_keys=n_keys),
        out_shape=jax.ShapeDtypeStruct((b, t, MLA_HEADS * V_HEAD), BF16),
        grid=(b, MLA_HEADS, t // tq),
        in_specs=[pl.BlockSpec((None, tq, QK_PAD), lambda bi, h, i: (bi, i, h)),
                  pl.BlockSpec((None, tkeys, QK_PAD), lambda bi, h, i: (bi, 0, h)),
                  pl.BlockSpec((None, tkeys, V_PAD), lambda bi, h, i: (bi, 0, h))],
        out_specs=pl.BlockSpec((None, tq, V_HEAD), lambda bi, h, i: (bi, i, h)),
        compiler_params=_cparams("parallel", "parallel", "arbitrary"),
        name="mla_attention",
    )(q, k, v)


def _rope_tables(pos):
    half = QK_ROPE // 2
    inv = ROPE_THETA ** (-jnp.arange(half, dtype=F32) / half)
    ang = pos.astype(F32)[:, None] * inv[None, :]
    cos, sin = jnp.cos(ang), jnp.sin(ang)
    t = pos.shape[0]
    cs_k = jnp.concatenate([cos, cos, -sin, sin], axis=1)
    one = jnp.ones((t, QK_NOPE), F32)
    zero_n = jnp.zeros((t, QK_NOPE), F32)
    zero_p = jnp.zeros((t, QK_PAD - QK_NOPE - QK_ROPE), F32)
    cos_q = jnp.concatenate([one, cos, cos, zero_p], axis=1)
    sin_q = jnp.concatenate([zero_n, -sin, sin, zero_p], axis=1)
    return cs_k, cos_q, sin_q


def _swap_halves(w):
    half = QK_ROPE // 2
    return jnp.concatenate([w[..., half:], w[..., :half]], axis=-1)


def _prep_weights(p):
    o1 = CONV_DIM + GDN_HEADS * GDN_DV
    w = {}
    w_in = p['gdn_w_in']
    w['gdn_qkv'] = w_in[:, :, :CONV_DIM].astype(BF16)
    w['gdn_gate'] = w_in[:, :, CONV_DIM:o1].astype(BF16)
    w['gdn_ab'] = jnp.pad(w_in[:, :, o1:], ((0, 0), (0, 0), (0, LANES - 2 * GDN_HEADS))).astype(BF16)
    w['gdn_o'] = p['gdn_w_o'].astype(BF16)
    w_dkv = p['mla_w_dkv']
    w['dkv_c'] = w_dkv[:, :KV_LORA].astype(BF16)
    rope_cols = w_dkv[:, KV_LORA:]
    w['dkv_r2'] = jnp.concatenate([rope_cols, _swap_halves(rope_cols)], axis=1).astype(BF16)
    w['uk'] = p['mla_w_uk'].astype(BF16)
    w['uv'] = p['mla_w_uv'].astype(BF16)
    w['dq'] = p['mla_w_dq'].astype(BF16)
    n_b = p['mla_w_uq'].shape[0]
    uq = p['mla_w_uq'].reshape(n_b, Q_LORA, MLA_HEADS, QK_NOPE + QK_ROPE)
    pad = jnp.zeros((n_b, Q_LORA, MLA_HEADS, QK_PAD - QK_NOPE - QK_ROPE), F32)
    w['uq'] = jnp.concatenate([uq, pad], axis=-1).reshape(n_b, Q_LORA, MLA_HEADS * QK_PAD).astype(BF16)
    uq_s = jnp.concatenate([jnp.zeros_like(uq[..., :QK_NOPE]), _swap_halves(uq[..., QK_NOPE:]), pad], axis=-1)
    w['uq_swap'] = uq_s.reshape(n_b, Q_LORA, MLA_HEADS * QK_PAD).astype(BF16)
    w['mla_o'] = p['mla_w_o'].astype(BF16)
    w['peer_q'] = p['peer_w_q'].astype(BF16)
    w['peer_keys'] = p['peer_sub_keys'].astype(BF16)
    w['peer_u'] = p['peer_u'].astype(BF16)
    pv = p['peer_v'].astype(BF16)
    pv = pv.reshape(pv.shape[0], N_EXPERTS // PEER_EXPERT_BLOCK, PEER_EXPERT_BLOCK, pv.shape[-1])
    w['peer_vt'] = jnp.swapaxes(pv, 2, 3)
    return w


def _trunk(x, mods, pos0, conv_bufs, delta_states, past_c, past_kr, p, w):
    b, t, d = x.shape
    depth = mods.shape[0]
    n_a = p['gdn_w_in'].shape[0]
    new_states, new_bufs = [], []
    c_kv = k_r = k_cat = v_all = None
    n_keys = past_c.shape[1] + t
    for l in range(depth):
        m = mods[l]
        if l < n_a:
            qkv, gate, ab = _norm_matmul(x, m, p['norm1_g'][l], [w['gdn_qkv'][l], w['gdn_gate'][l], w['gdn_ab'][l]],
                                         [F32, F32, F32], mod_off=0)
            o, s = _gdn(qkv, gate, ab, conv_bufs[l], delta_states[l], p['gdn_conv_w'][l], p['gdn_a_log'][l],
                        p['gdn_dt_bias'][l], p['gdn_onorm_g'][l])
            new_bufs.append(qkv[:, t - (CONV_W - 1):, :])
            new_states.append(s)
            w_o = w['gdn_o'][l]
        else:
            j = l - n_a
            cs_k, cos_q, sin_q = _rope_tables(pos0 + jnp.arange(t))
            if j == 0:
                c_kv, k_r = _shared_latent(x, p['kv_src_g'], w['dkv_c'], w['dkv_r2'], cs_k, p['mla_kv_norm_g'])
                c_all = jnp.concatenate([past_c, c_kv], axis=1)
                kr_all = jnp.concatenate([past_kr, k_r], axis=1)
                pad = -n_keys % LANES
                c_all = jnp.pad(c_all, ((0, 0), (0, pad), (0, 0)))
                kr_all = jnp.pad(kr_all, ((0, 0), (0, pad), (0, 0)))
                k_cat, v_all = _mla_keys_values(c_all, kr_all, w['uk'], w['uv'])
            q = _mla_queries(x, m, p['norm1_g'][l], w['dq'][j], p['mla_q_norm_g'][j], w['uq'][j], w['uq_swap'][j],
                             cos_q, sin_q)
            o = _attention(q, k_cat, v_all, pos0=pos0, n_keys=n_keys)
            w_o = w['mla_o'][j]
        x, h2 = _out_residual(o, w_o, x, m, p['norm2_g'][l])
        x = _peer(h2, x, m, w['peer_q'][l], w['peer_keys'][l], w['peer_u'][l], w['peer_vt'][l], p['final_g'],
                  final_norm=(l == depth - 1))
    return x, jnp.stack(new_states), jnp.stack(new_bufs), c_kv, k_r


def kernel(x_prompt, x_sample, c_prompt, c_sample, state_delta, state_conv, cache_kv_latent, cache_k_rope, w_mod, b_mod, norm1_g, norm2_g, final_g, gdn_w_in, gdn_conv_w, gdn_a_log, gdn_dt_bias, gdn_onorm_g, gdn_w_o, kv_src_g, mla_w_dkv, mla_kv_norm_g, mla_w_uk, mla_w_uv, mla_w_dq, mla_q_norm_g, mla_w_uq, mla_w_o, peer_w_q, peer_sub_keys, peer_u, peer_v):
    p = dict(norm1_g=norm1_g, norm2_g=norm2_g, final_g=final_g, gdn_w_in=gdn_w_in, gdn_conv_w=gdn_conv_w,
             gdn_a_log=gdn_a_log, gdn_dt_bias=gdn_dt_bias, gdn_onorm_g=gdn_onorm_g, gdn_w_o=gdn_w_o,
             kv_src_g=kv_src_g, mla_w_dkv=mla_w_dkv, mla_kv_norm_g=mla_kv_norm_g, mla_w_uk=mla_w_uk,
             mla_w_uv=mla_w_uv, mla_w_dq=mla_w_dq, mla_q_norm_g=mla_q_norm_g, mla_w_uq=mla_w_uq, mla_w_o=mla_w_o,
             peer_w_q=peer_w_q, peer_sub_keys=peer_sub_keys, peer_u=peer_u, peer_v=peer_v)
    w = _prep_weights(p)
    bp, tp, d = x_prompt.shape
    bs, ts, _ = x_sample.shape
    n_a = gdn_w_in.shape[0]

    rows = -(-(bp + bs) // SUBLANES) * SUBLANES
    c_all = jnp.pad(jnp.concatenate([c_prompt, c_sample], axis=0), ((0, rows - bp - bs), (0, 0)))
    mods = _modulation(c_all, w_mod, b_mod)
    mods_p = mods[:, :bp, None, :]
    mods_s = mods[:, bp:bp + bs, None, :]

    zero_bufs = jnp.zeros((n_a, bp, CONV_W - 1, CONV_DIM), F32)
    zero_states = jnp.zeros((n_a, bp, GDN_HEADS, GDN_DK, GDN_DV), F32)
    no_c = jnp.zeros((bp, 0, KV_LORA), F32)
    no_kr = jnp.zeros((bp, 0, QK_ROPE), F32)
    y_p, p_delta, p_conv, p_lat, p_kr = _trunk(x_prompt, mods_p, 0, zero_bufs, zero_states, no_c, no_kr, p, w)
    past = cache_kv_latent.shape[1]
    y_s, s_delta, s_conv, s_lat, s_kr = _trunk(x_sample, mods_s, past, state_conv, state_delta, cache_kv_latent,
                                               cache_k_rope, p, w)
    return (y_p, y_s, p_delta, p_conv, p_lat, p_kr, s_delta, s_conv, s_lat, s_kr)
```

```python
import functools
import math

import jax
import jax.numpy as jnp
from jax import lax
from jax.experimental import pallas as pl
from jax.experimental.pallas import tpu as pltpu

F32 = jnp.float32
BF16 = jnp.bfloat16
HIGHEST = lax.Precision.HIGHEST

D_MODEL = 1024
EPS = 1e-6
CHUNK = 64

GDN_HEADS = 8
GDN_DK = 128
GDN_DV = 128
CONV_W = 4
GDN_QK = GDN_HEADS * GDN_DK
CONV_DIM = GDN_HEADS * (2 * GDN_DK + GDN_DV)

MLA_HEADS = 8
QK_NOPE = 128
QK_ROPE = 64
V_HEAD = 128
KV_LORA = 256
Q_LORA = 384
ROPE_THETA = 10000.0
SM_SCALE = (QK_NOPE + QK_ROPE) ** -0.5
QK_PAD = 256
V_PAD = 256
ATTN_KEY_BLOCK = 1024

N_KEYS = 128
N_EXPERTS = N_KEYS * N_KEYS
PEER_HEADS = 8
PEER_TOPK = 16
PK_HALF = 128
PEER_EXPERT_BLOCK = 512
PEER_BLOCKS_PER_STEP = 4

LANES = 128
SUBLANES = 8
BF16_ROWS = 16
VMEM_LIMIT = 56 * 1024 * 1024


def _cparams(*sem):
    return pltpu.CompilerParams(dimension_semantics=sem, vmem_limit_bytes=VMEM_LIMIT)


def _silu(x):
    return x * jax.nn.sigmoid(x)


def _softplus(x):
    return jnp.maximum(x, 0.0) + jnp.log1p(jnp.exp(-jnp.abs(x)))


def _rms(x):
    return x * lax.rsqrt(jnp.mean(x * x, axis=-1, keepdims=True) + EPS)


def _dot(a, b, **kw):
    return jnp.dot(a, b, preferred_element_type=F32, **kw)


def _dot_nt(a, b, **kw):
    return lax.dot_general(a, b, (((1,), (1,)), ((), ())), preferred_element_type=F32, **kw)


def _dot_tn(a, b, **kw):
    return lax.dot_general(a, b, (((0,), (0,)), ((), ())), preferred_element_type=F32, **kw)


def _split_bf16(a):
    hi = a.astype(BF16)
    return hi, (a - hi.astype(F32)).astype(BF16)


def _dot3(a, b):
    ah, al = _split_bf16(a)
    bh, bl = _split_bf16(b)
    return _dot(jnp.concatenate([ah, ah, al], axis=1), jnp.concatenate([bh, bl, bh], axis=0))


def _row_block(t, cap):
    b = min(t, cap)
    while t % b:
        b //= 2
    return b


def _mod_kernel(c_ref, w_ref, b_ref, o_ref):
    a = _silu(c_ref[...]).astype(BF16)
    o_ref[...] = _dot(a, w_ref[...].astype(BF16)) + b_ref[...]


def _modulation(c_pad, w_mod, b_mod):
    depth, d, n = w_mod.shape
    r = c_pad.shape[0]
    tn = 1024
    return pl.pallas_call(
        _mod_kernel,
        out_shape=jax.ShapeDtypeStruct((depth, r, n), F32),
        grid=(depth, n // tn),
        in_specs=[pl.BlockSpec((r, d), lambda l, j: (0, 0)),
                  pl.BlockSpec((None, d, tn), lambda l, j: (l, 0, j)),
                  pl.BlockSpec((None, 1, tn), lambda l, j: (l, 0, j))],
        out_specs=pl.BlockSpec((None, r, tn), lambda l, j: (l, 0, j)),
        compiler_params=_cparams("arbitrary", "arbitrary"),
        name="adaln_modulation",
    )(c_pad, w_mod, b_mod.reshape(depth, 1, n))


def _norm_mm_kernel(*refs, n_w, mod_off):
    x_ref, m_ref, g_ref = refs[:3]
    w_refs = refs[3:3 + n_w]
    o_refs = refs[3 + n_w:]
    y = _rms(x_ref[...]) * g_ref[...]
    if mod_off is not None:
        sh = m_ref[:, mod_off * D_MODEL:(mod_off + 1) * D_MODEL]
        sc = m_ref[:, (mod_off + 1) * D_MODEL:(mod_off + 2) * D_MODEL]
        y = y * (1.0 + sc) + sh
    yb = y.astype(BF16)
    for w_ref, o_ref in zip(w_refs, o_refs):
        o_ref[...] = _dot(yb, w_ref[...]).astype(o_ref.dtype)


def _norm_matmul(x, mods, g, weights, out_dtypes, *, mod_off, block_cap=512):
    b, t, d = x.shape
    tm = _row_block(t, block_cap)
    nb = t // tm
    n = b * t
    x2 = x.reshape(n, d)
    in_specs = [pl.BlockSpec((tm, d), lambda i: (i, 0)),
                pl.BlockSpec((None, 1, 6 * d), lambda i: (i // nb, 0, 0)),
                pl.BlockSpec((1, d), lambda i: (0, 0))]
    in_specs += [pl.BlockSpec(w.shape, lambda i: (0, 0)) for w in weights]
    outs = pl.pallas_call(
        functools.partial(_norm_mm_kernel, n_w=len(weights), mod_off=mod_off),
        out_shape=[jax.ShapeDtypeStruct((n, w.shape[1]), dt) for w, dt in zip(weights, out_dtypes)],
        grid=(n // tm,),
        in_specs=in_specs,
        out_specs=[pl.BlockSpec((tm, w.shape[1]), lambda i: (i, 0)) for w in weights],
        compiler_params=_cparams("parallel"),
        name="norm_matmul",
    )(x2, mods, g.reshape(1, d), *weights)
    return [o.reshape(b, t, o.shape[-1]) for o in outs]


def _gdn_kernel(qkv_ref, gate_ref, ab_ref, cbuf_ref, s0_ref, cw_ref, par_ref, og_ref,
                o_ref, s_ref, xp_ref, *, chunk):
    c = chunk
    step = pl.program_id(1)

    @pl.when(step == 0)
    def _():
        s_ref[...] = s0_ref[...]
        xp_ref[0:SUBLANES, :] = cbuf_ref[...]

    xp_ref[SUBLANES:SUBLANES + c, :] = qkv_ref[...]
    base = SUBLANES - (CONV_W - 1)
    conv = xp_ref[base:base + c, :] * cw_ref[0:1, :]
    for w in range(1, CONV_W):
        conv = conv + xp_ref[base + w:base + w + c, :] * cw_ref[w:w + 1, :]
    xp_ref[0:SUBLANES, :] = xp_ref[c:c + SUBLANES, :]
    act = _silu(conv)

    ab = ab_ref[...]
    g_all = -jnp.exp(par_ref[0:1, :]) * _softplus(ab + par_ref[1:2, :])
    beta_all = jax.nn.sigmoid(ab)

    row = lax.broadcasted_iota(jnp.int32, (c, c), 0)
    col = lax.broadcasted_iota(jnp.int32, (c, c), 1)
    causal = row >= col
    strict = row > col
    eye = (row == col).astype(F32)
    lower = causal.astype(F32)
    upper = (row <= col).astype(F32)

    d_col_all = _dot(lower, g_all, precision=HIGHEST)
    d_row_all = _dot_tn(g_all, upper, precision=HIGHEST)

    n_double = int(math.log2(c)) - 1
    keep_s = lax.broadcasted_iota(jnp.int32, (c, 2 * c), 1) < c
    heads = range(GDN_HEADS)
    q, k, v, beta, d_col, d_last, decay, e_col, kb, kq = ([None] * GDN_HEADS for _ in range(10))
    for h in heads:
        qh = act[:, h * GDN_DK:(h + 1) * GDN_DK]
        kh = act[:, GDN_QK + h * GDN_DK:GDN_QK + (h + 1) * GDN_DK]
        v[h] = act[:, 2 * GDN_QK + h * GDN_DV:2 * GDN_QK + (h + 1) * GDN_DV]
        q[h] = qh * lax.rsqrt(jnp.sum(qh * qh, axis=-1, keepdims=True) + EPS) * (GDN_DK ** -0.5)
        k[h] = kh * lax.rsqrt(jnp.sum(kh * kh, axis=-1, keepdims=True) + EPS)
        beta[h] = beta_all[:, GDN_HEADS + h:GDN_HEADS + h + 1]
        d_col[h] = d_col_all[:, h:h + 1]
        d_last[h] = d_col_all[c - 1:c, h:h + 1]
        decay[h] = jnp.exp(jnp.where(causal, d_col[h] - d_row_all[h:h + 1, :], -jnp.inf))
        e_col[h] = jnp.exp(d_col[h])
        kb[h] = k[h] * beta[h]
    for h in heads:
        kq[h] = _dot_nt(jnp.concatenate([kb[h], q[h]], axis=0).astype(BF16), k[h].astype(BF16))
    m = [-jnp.where(strict, kq[h][:c] * decay[h], 0.0) for h in heads]
    x = [jnp.concatenate([eye, m[h]], axis=1) for h in heads]
    for _ in range(n_double):
        for h in heads:
            x[h] = jnp.where(keep_s, x[h], 0.0) + _dot3(m[h], x[h])
            m[h] = x[h][:, c:]
    p = [x[h][:, :c] + _dot3(m[h], x[h][:, :c]) for h in heads]
    sol = [_dot3(p[h], jnp.concatenate([v[h] * beta[h], kb[h] * e_col[h]], axis=1)) for h in heads]
    s = [s_ref[h] for h in heads]
    ws = [_dot(jnp.concatenate([sol[h][:, GDN_DV:], q[h] * e_col[h]], axis=0).astype(BF16), s[h].astype(BF16))
          for h in heads]
    vnb = [(sol[h][:, :GDN_DV] - ws[h][:c]).astype(BF16) for h in heads]
    o = [ws[h][c:] + _dot((kq[h][c:] * decay[h]).astype(BF16), vnb[h]) for h in heads]
    for h in heads:
        k_dec = (k[h] * jnp.exp(d_last[h] - d_col[h])).astype(BF16)
        s_ref[h] = s[h] * jnp.exp(d_last[h]) + _dot_tn(k_dec, vnb[h])
    for h in heads:
        gate = gate_ref[:, h * GDN_DV:(h + 1) * GDN_DV]
        o_ref[:, h * GDN_DV:(h + 1) * GDN_DV] = (_rms(o[h]) * og_ref[...] * _silu(gate)).astype(o_ref.dtype)


def _gdn(qkv, gate, ab, conv_buf, s0, conv_w, a_log, dt_bias, onorm_g):
    b, t, _ = qkv.shape
    c = min(CHUNK, t)
    cbuf = jnp.pad(conv_buf, ((0, 0), (SUBLANES - (CONV_W - 1), 0), (0, 0)))
    cw = jnp.pad(conv_w, ((0, SUBLANES - CONV_W), (0, 0)))
    par = jnp.zeros((SUBLANES, LANES), F32)
    par = par.at[0, :GDN_HEADS].set(a_log).at[1, :GDN_HEADS].set(dt_bias)
    o, s = pl.pallas_call(
        functools.partial(_gdn_kernel, chunk=c),
        out_shape=[jax.ShapeDtypeStruct((b, t, GDN_HEADS * GDN_DV), BF16),
                   jax.ShapeDtypeStruct((b, GDN_HEADS, GDN_DK, GDN_DV), F32)],
        grid=(b, t // c),
        in_specs=[pl.BlockSpec((None, c, CONV_DIM), lambda i, j: (i, j, 0)),
                  pl.BlockSpec((None, c, GDN_HEADS * GDN_DV), lambda i, j: (i, j, 0)),
                  pl.BlockSpec((None, c, LANES), lambda i, j: (i, j, 0)),
                  pl.BlockSpec((None, SUBLANES, CONV_DIM), lambda i, j: (i, 0, 0)),
                  pl.BlockSpec((None, GDN_HEADS, GDN_DK, GDN_DV), lambda i, j: (i, 0, 0, 0)),
                  pl.BlockSpec((SUBLANES, CONV_DIM), lambda i, j: (0, 0)),
                  pl.BlockSpec((SUBLANES, LANES), lambda i, j: (0, 0)),
                  pl.BlockSpec((1, GDN_DV), lambda i, j: (0, 0))],
        out_specs=[pl.BlockSpec((None, c, GDN_HEADS * GDN_DV), lambda i, j: (i, j, 0)),
                   pl.BlockSpec((None, GDN_HEADS, GDN_DK, GDN_DV), lambda i, j: (i, 0, 0, 0))],
        scratch_shapes=[pltpu.VMEM((c + SUBLANES, CONV_DIM), F32)],
        compiler_params=_cparams("parallel", "arbitrary"),
        name="gated_delta",
    )(qkv, gate, ab, cbuf, s0, cw, par, onorm_g.reshape(1, GDN_DV))
    return o, s


def _out_res_kernel(o_ref, w_ref, x_ref, m_ref, g2_ref, y_ref, h2_ref):
    def mod(i):
        return m_ref[:, i * D_MODEL:(i + 1) * D_MODEL]

    y = x_ref[...] + mod(2) * _dot(o_ref[...], w_ref[...])
    y_ref[...] = y
    h2_ref[...] = (_rms(y) * g2_ref[...] * (1.0 + mod(4)) + mod(3)).astype(h2_ref.dtype)


def _out_residual(o, w, x, mods, norm2_g, *, block_cap=512):
    b, t, d = x.shape
    kdim = o.shape[-1]
    tm = _row_block(t, block_cap)
    nb = t // tm
    n = b * t
    y, h2 = pl.pallas_call(
        _out_res_kernel,
        out_shape=[jax.ShapeDtypeStruct((n, d), F32), jax.ShapeDtypeStruct((n, d), BF16)],
        grid=(n // tm,),
        in_specs=[pl.BlockSpec((tm, kdim), lambda i: (i, 0)),
                  pl.BlockSpec((kdim, d), lambda i: (0, 0)),
                  pl.BlockSpec((tm, d), lambda i: (i, 0)),
                  pl.BlockSpec((None, 1, 6 * d), lambda i: (i // nb, 0, 0)),
                  pl.BlockSpec((1, d), lambda i: (0, 0))],
        out_specs=[pl.BlockSpec((tm, d), lambda i: (i, 0)), pl.BlockSpec((tm, d), lambda i: (i, 0))],
        compiler_params=_cparams("parallel"),
        name="out_proj_residual",
    )(o.reshape(n, kdim), w, x.reshape(n, d), mods, norm2_g.reshape(1, d))
    return y.reshape(b, t, d), h2


_CAND = [(j1, j2) for j1 in range(PEER_TOPK) for j2 in range(PEER_TOPK) if (j1 + 1) * (j2 + 1) <= PEER_TOPK]
_CAND_PAD = -(-len(_CAND) // SUBLANES) * SUBLANES


def _top_values(s, n_top):
    vals = [jnp.max(s, axis=0, keepdims=True)]
    for _ in range(1, n_top):
        vals.append(jnp.max(jnp.where(s < vals[-1], s, -jnp.inf), axis=0, keepdims=True))
    return vals


def _sort_network(n):
    pairs = []
    p = 1
    while p < n:
        k = p
        while k >= 1:
            for j in range(k % p, n - k, 2 * k):
                for i in range(min(k, n - j - k)):
                    if (i + j) // (2 * p) == (i + j + k) // (2 * p):
                        pairs.append((i + j, i + j + k))
            k //= 2
        p *= 2
    return pairs


def _top_values_keys(s, n_top):
    groups = s.shape[0] // SUBLANES
    assert groups == n_top
    r = [s[g * SUBLANES:(g + 1) * SUBLANES, :] for g in range(groups)]
    for lo, hi in _sort_network(groups):
        r[lo], r[hi] = jnp.maximum(r[lo], r[hi]), jnp.minimum(r[lo], r[hi])
    vals = []
    for t in range(n_top):
        m = jnp.max(r[0], axis=0, keepdims=True)
        vals.append(m)
        if t + 1 < n_top:
            pop = r[0] == m
            for k in range(n_top - 1 - t):
                r[k] = jnp.where(pop, r[k + 1], r[k])
    return vals


def _rank_among(s, vals):
    rank = jnp.zeros(s.shape, F32)
    for v in vals:
        rank = rank + jnp.where(s < v, 1.0, 0.0)
    return rank


def _peer_route_kernel(h2_ref, wq_ref, keys_ref, rank2_ref, e2_ref, n1_ref, e1_ref, q_scr):
    q_scr[...] = _dot(h2_ref[...], wq_ref[...]).astype(BF16)
    tm = h2_ref.shape[0]

    def head(h, carry):
        off = pl.multiple_of(h * 2 * PK_HALF, 2 * PK_HALF)
        s1 = _dot_nt(keys_ref[h, 0], q_scr[:, pl.ds(off, PK_HALF)])
        s2 = _dot_nt(keys_ref[h, 1], q_scr[:, pl.ds(off + PK_HALF, PK_HALF)])
        v1 = _top_values_keys(s1, PEER_TOPK)
        v2 = _top_values_keys(s2, PEER_TOPK)
        rank2 = _rank_among(s2, v2)
        rows = [v1[j1] + v2[j2] for j1, j2 in _CAND]
        rows += [jnp.full((1, tm), -jnp.inf, F32)] * (_CAND_PAD - len(_CAND))
        cand = jnp.concatenate(rows, axis=0)
        tau = _top_values(cand, PEER_TOPK)[-1]
        sel = jnp.where(cand >= tau, 1.0, 0.0)
        z = jnp.sum(sel * jnp.exp(cand - cand[0:1, :]), axis=0, keepdims=True)
        n1 = jnp.zeros(s1.shape, F32)
        off_c = 0
        for j1 in range(PEER_TOPK):
            cnt = PEER_TOPK // (j1 + 1)
            n_j = jnp.sum(sel[off_c:off_c + cnt, :], axis=0, keepdims=True)
            n1 = jnp.where(s1 == v1[j1], n_j, n1)
            off_c += cnt
        rank2_ref[h] = rank2.astype(rank2_ref.dtype)
        e2_ref[h] = jnp.exp(s2 - v2[0]).astype(e2_ref.dtype)
        n1_ref[h] = n1
        e1_ref[h] = jnp.exp(s1 - v1[0]) * (1.0 / z)
        return carry

    lax.fori_loop(0, PEER_HEADS, head, 0)


def _peer_route(h2, w_q, sub_keys, *, tm):
    n, d = h2.shape
    shp = (PEER_HEADS, N_KEYS, n)
    blk = pl.BlockSpec((PEER_HEADS, N_KEYS, tm), lambda i: (0, 0, i))
    return pl.pallas_call(
        _peer_route_kernel,
        out_shape=[jax.ShapeDtypeStruct(shp, BF16), jax.ShapeDtypeStruct(shp, BF16),
                   jax.ShapeDtypeStruct(shp, F32), jax.ShapeDtypeStruct(shp, F32)],
        grid=(n // tm,),
        in_specs=[pl.BlockSpec((tm, d), lambda i: (i, 0)),
                  pl.BlockSpec(w_q.shape, lambda i: (0, 0)),
                  pl.BlockSpec(sub_keys.shape, lambda i: (0, 0, 0, 0))],
        out_specs=[blk, blk, blk, blk],
        scratch_shapes=[pltpu.VMEM((tm, w_q.shape[1]), BF16)],
        compiler_params=_cparams("parallel"),
        name="peer_route",
    )(h2, w_q, sub_keys)


def _peer_dense_kernel(*refs, eb, halves, final_norm):
    h2_ref, u_ref = refs[:2]
    vt_refs = refs[2:2 + halves]
    (rank2_ref, e2_ref, n1_ref, e1_ref, x_ref, m_ref, fg_ref, y_ref, acc_a_ref, acc_b_ref, wa_a_ref, wa_b_ref,
     gate_a_ref, gate_b_ref, nb_ref, eb_ref) = refs[2 + halves:]
    vt_prev_ref = vt_refs[0]
    acc_refs, wa_refs, gate_refs = (acc_a_ref, acc_b_ref), (wa_a_ref, wa_b_ref), (gate_a_ref, gate_b_ref)
    j = pl.program_id(1)
    n_pairs = pl.num_programs(1) - 1
    rows_per_half = eb // N_KEYS
    tm = h2_ref.shape[0]
    zero = jnp.zeros((), BF16)

    @pl.when(j == 0)
    def _():
        acc_a_ref[...] = jnp.zeros_like(acc_a_ref)
        acc_b_ref[...] = jnp.zeros_like(acc_b_ref)
        wa_b_ref[...] = jnp.zeros_like(wa_b_ref)

    tn = tm // 2
    s_tiles = list(range(0, N_KEYS, BF16_ROWS))

    def stage_rows(half):
        for h in range(PEER_HEADS):
            n_all = n1_ref[h].astype(BF16)
            e_all = e1_ref[h].astype(BF16)
            for r in range(rows_per_half):
                k = (half * PEER_HEADS + h) * rows_per_half + r
                row = half * rows_per_half + r
                nb_ref[k] = jnp.broadcast_to(n_all[row:row + 1, :], (BF16_ROWS, tm))
                eb_ref[k] = jnp.broadcast_to(e_all[row:row + 1, :], (BF16_ROWS, tm))

    def gate_tiles(half, gate_ref, tiles):
        for s0 in tiles:
            w = [None] * rows_per_half
            for h in range(PEER_HEADS):
                r2 = rank2_ref[h, s0:s0 + BF16_ROWS, :]
                e2 = e2_ref[h, s0:s0 + BF16_ROWS, :]
                for r in range(rows_per_half):
                    k = (half * PEER_HEADS + h) * rows_per_half + r
                    term = jnp.where(r2 < nb_ref[k], e2, zero) * eb_ref[k]
                    w[r] = term if w[r] is None else w[r] + term
            for r in range(rows_per_half):
                gate_ref[r * N_KEYS + s0:r * N_KEYS + s0 + BF16_ROWS, :] = w[r]

    def pre_act(half, piece, wa_ref):
        lanes = slice(piece * tn, (piece + 1) * tn)
        wa_ref[:, lanes] = _dot_nt(u_ref[half * eb:(half + 1) * eb, :], h2_ref[lanes, :]).astype(BF16)

    def activate(piece, gate_ref, wa_ref):
        lanes = slice(piece * tn, (piece + 1) * tn)
        wa_ref[:, lanes] = jax.nn.gelu(wa_ref[:, lanes]) * gate_ref[:, lanes]

    def value_matmul(vt_ref, wa_ref, acc_ref, piece):
        lanes = slice(piece * tn, (piece + 1) * tn)
        acc_ref[:, lanes] += _dot(vt_ref[...], wa_ref[:, lanes])

    @pl.when(j < n_pairs)
    def _():
        for hf in range(halves):
            stage_rows(hf)
        for hf in range(halves):
            wa_ref, gate_ref = wa_refs[hf % 2], gate_refs[hf % 2]
            lag = (hf - 1) % 2
            pre_act(hf, 0, wa_ref)
            gate_tiles(hf, gate_ref, s_tiles[:4])
            pre_act(hf, 1, wa_ref)
            gate_tiles(hf, gate_ref, s_tiles[4:])
            value_matmul(vt_refs[hf], wa_refs[lag], acc_refs[lag], 0)
            activate(0, gate_ref, wa_ref)
            value_matmul(vt_refs[hf], wa_refs[lag], acc_refs[lag], 1)
            activate(1, gate_ref, wa_ref)

    @pl.when(j == n_pairs)
    def _():
        nseq, _, _ = m_ref.shape
        d = x_ref.shape[1]
        g2 = m_ref[:, :, 5 * D_MODEL:6 * D_MODEL]
        acc = acc_a_ref[...] + acc_b_ref[...] + _dot(vt_prev_ref[...], wa_b_ref[...])
        out = acc.T.reshape(nseq, tm // nseq, d)
        y = (x_ref[...].reshape(nseq, tm // nseq, d) + g2 * out).reshape(tm, d)
        if final_norm:
            y = _rms(y) * fg_ref[...]
        y_ref[...] = y


def _peer_dense(h2, route, u_b, vt_b, x, mods, final_g, *, tm, eb, final_norm):
    b, t, d = x.shape
    n = b * t
    nseq = max(1, tm // t)
    rank2, e2, n1, e1 = route
    tok = pl.BlockSpec((PEER_HEADS, N_KEYS, tm), lambda i, j: (0, 0, i))
    halves = PEER_BLOCKS_PER_STEP
    n_blocks = N_EXPERTS // eb
    n_pairs = n_blocks // halves
    n_bcast = halves * PEER_HEADS * (eb // N_KEYS)
    key_rows = pl.BlockSpec((PEER_HEADS, halves * eb // N_KEYS, tm),
                            lambda i, j: (0, jnp.minimum(j, n_pairs - 1), i))

    def vt_spec(k):
        return pl.BlockSpec((None, d, eb), lambda i, j: (jnp.clip(halves * j + k - 1, 0, n_blocks - 1), 0, 0))

    y = pl.pallas_call(
        functools.partial(_peer_dense_kernel, eb=eb, halves=halves, final_norm=final_norm),
        out_shape=jax.ShapeDtypeStruct((n, d), F32),
        grid=(n // tm, n_pairs + 1),
        in_specs=[pl.BlockSpec((tm, d), lambda i, j: (i, 0)),
                  pl.BlockSpec((halves * eb, d), lambda i, j: (jnp.minimum(j, n_pairs - 1), 0))]
                 + [vt_spec(k) for k in range(halves)] +
                 [tok, tok, key_rows, key_rows,
                  pl.BlockSpec((tm, d), lambda i, j: (i, 0)),
                  pl.BlockSpec((nseq, 1, 6 * d), lambda i, j: ((i * tm) // t // nseq, 0, 0)),
                  pl.BlockSpec((1, d), lambda i, j: (0, 0))],
        out_specs=pl.BlockSpec((tm, d), lambda i, j: (i, 0)),
        scratch_shapes=[pltpu.VMEM((d, tm), F32), pltpu.VMEM((d, tm), F32),
                        pltpu.VMEM((eb, tm), BF16), pltpu.VMEM((eb, tm), BF16),
                        pltpu.VMEM((eb, tm), BF16), pltpu.VMEM((eb, tm), BF16),
                        pltpu.VMEM((n_bcast, BF16_ROWS, tm), BF16), pltpu.VMEM((n_bcast, BF16_ROWS, tm), BF16)],
        compiler_params=_cparams("parallel", "arbitrary"),
        name="peer_experts",
    )(h2, u_b, *([vt_b] * halves), rank2, e2, n1, e1, x.reshape(n, d), mods, final_g.reshape(1, d))
    return y.reshape(b, t, d)


def _peer(h2, x, mods, w_q, sub_keys, u_b, vt_b, final_g, *, final_norm):
    n = h2.shape[0]
    tm = _row_block(n, 512)
    route = _peer_route(h2, w_q, sub_keys, tm=min(tm, 256))
    return _peer_dense(h2, route, u_b, vt_b, x, mods, final_g, tm=tm, eb=PEER_EXPERT_BLOCK, final_norm=final_norm)


def _latent_kernel(x_ref, g_ref, wc_ref, wr_ref, cs_ref, kvg_ref, c_ref, kr_ref):
    xn = (_rms(x_ref[...]) * g_ref[...]).astype(BF16)
    c_ref[...] = _rms(_dot(xn, wc_ref[...])) * kvg_ref[...]
    p = _dot(xn, wr_ref[...]) * cs_ref[...]
    kr_ref[...] = p[:, :QK_ROPE] + p[:, QK_ROPE:]


def _shared_latent(x, kv_src_g, w_c, w_r2, cs_k, kv_norm_g, *, block_cap=512):
    b, t, d = x.shape
    tm = _row_block(t, block_cap)
    nb = t // tm
    n = b * t
    c_kv, k_r = pl.pallas_call(
        _latent_kernel,
        out_shape=[jax.ShapeDtypeStruct((n, KV_LORA), F32), jax.ShapeDtypeStruct((n, QK_ROPE), F32)],
        grid=(n // tm,),
        in_specs=[pl.BlockSpec((tm, d), lambda i: (i, 0)),
                  pl.BlockSpec((1, d), lambda i: (0, 0)),
                  pl.BlockSpec(w_c.shape, lambda i: (0, 0)),
                  pl.BlockSpec(w_r2.shape, lambda i: (0, 0)),
                  pl.BlockSpec((tm, 2 * QK_ROPE), lambda i: (i % nb, 0)),
                  pl.BlockSpec((1, KV_LORA), lambda i: (0, 0))],
        out_specs=[pl.BlockSpec((tm, KV_LORA), lambda i: (i, 0)), pl.BlockSpec((tm, QK_ROPE), lambda i: (i, 0))],
        compiler_params=_cparams("parallel"),
        name="mla_shared_latent",
    )(x.reshape(n, d), kv_src_g.reshape(1, d), w_c, w_r2, cs_k, kv_norm_g.reshape(1, KV_LORA))
    return c_kv.reshape(b, t, KV_LORA), k_r.reshape(b, t, QK_ROPE)


def _query_kernel(x_ref, m_ref, g_ref, wdq_ref, qg_ref, wq_ref, wqs_ref, c_ref, s_ref, q_ref):
    y = _rms(x_ref[...]) * g_ref[...]
    y = y * (1.0 + m_ref[:, D_MODEL:2 * D_MODEL]) + m_ref[:, 0:D_MODEL]
    cq = (_rms(_dot(y.astype(BF16), wdq_ref[...])) * qg_ref[...]).astype(BF16)
    q = _dot(cq, wq_ref[...])
    qs = _dot(cq, wqs_ref[...])
    cos = c_ref[...]
    sin = s_ref[...]
    for h in range(MLA_HEADS):
        sl = slice(h * QK_PAD, (h + 1) * QK_PAD)
        q_ref[:, sl] = (q[:, sl] * cos + qs[:, sl] * sin).astype(q_ref.dtype)


def _mla_queries(x, mods, norm1_g, w_dq, q_norm_g, w_q, w_qs, cos_q, sin_q, *, block_cap=512):
    b, t, d = x.shape
    tm = _row_block(t, block_cap)
    nb = t // tm
    n = b * t
    q = pl.pallas_call(
        _query_kernel,
        out_shape=jax.ShapeDtypeStruct((n, MLA_HEADS * QK_PAD), BF16),
        grid=(n // tm,),
        in_specs=[pl.BlockSpec((tm, d), lambda i: (i, 0)),
                  pl.BlockSpec((None, 1, 6 * d), lambda i: (i // nb, 0, 0)),
                  pl.BlockSpec((1, d), lambda i: (0, 0)),
                  pl.BlockSpec(w_dq.shape, lambda i: (0, 0)),
                  pl.BlockSpec((1, Q_LORA), lambda i: (0, 0)),
                  pl.BlockSpec(w_q.shape, lambda i: (0, 0)),
                  pl.BlockSpec(w_qs.shape, lambda i: (0, 0)),
                  pl.BlockSpec((tm, QK_PAD), lambda i: (i % nb, 0)),
                  pl.BlockSpec((tm, QK_PAD), lambda i: (i % nb, 0))],
        out_specs=pl.BlockSpec((tm, MLA_HEADS * QK_PAD), lambda i: (i, 0)),
        compiler_params=_cparams("parallel"),
        name="mla_queries",
    )(x.reshape(n, d), mods, norm1_g.reshape(1, d), w_dq, q_norm_g.reshape(1, Q_LORA), w_q, w_qs, cos_q, sin_q)
    return q.reshape(b, t, MLA_HEADS * QK_PAD)


def _kv_kernel(c_ref, kr_ref, wuk_ref, wuv_ref, k_ref, v_ref):
    cb = c_ref[...].astype(BF16)
    kn = _dot(cb, wuk_ref[...])
    vv = _dot(cb, wuv_ref[...])
    tm = c_ref.shape[0]
    tail = jnp.concatenate([kr_ref[...], jnp.zeros((tm, QK_PAD - QK_NOPE - QK_ROPE), F32)], axis=1)
    tail = tail.astype(k_ref.dtype)
    ones_col = (lax.broadcasted_iota(jnp.int32, (tm, V_PAD - V_HEAD), 1) == 0).astype(v_ref.dtype)
    for h in range(MLA_HEADS):
        k_ref[:, h * QK_PAD:h * QK_PAD + QK_NOPE] = kn[:, h * QK_NOPE:(h + 1) * QK_NOPE].astype(k_ref.dtype)
        k_ref[:, h * QK_PAD + QK_NOPE:(h + 1) * QK_PAD] = tail
        v_ref[:, h * V_PAD:h * V_PAD + V_HEAD] = vv[:, h * V_HEAD:(h + 1) * V_HEAD].astype(v_ref.dtype)
        v_ref[:, h * V_PAD + V_HEAD:(h + 1) * V_PAD] = ones_col


def _mla_keys_values(c_all, kr_all, w_uk, w_uv, *, block_cap=512):
    b, tk, _ = c_all.shape
    n = b * tk
    tm = _row_block(n, block_cap)
    k, v = pl.pallas_call(
        _kv_kernel,
        out_shape=[jax.ShapeDtypeStruct((n, MLA_HEADS * QK_PAD), BF16),
                   jax.ShapeDtypeStruct((n, MLA_HEADS * V_PAD), BF16)],
        grid=(n // tm,),
        in_specs=[pl.BlockSpec((tm, KV_LORA), lambda i: (i, 0)),
                  pl.BlockSpec((tm, QK_ROPE), lambda i: (i, 0)),
                  pl.BlockSpec(w_uk.shape, lambda i: (0, 0)),
                  pl.BlockSpec(w_uv.shape, lambda i: (0, 0))],
        out_specs=[pl.BlockSpec((tm, MLA_HEADS * QK_PAD), lambda i: (i, 0)),
                   pl.BlockSpec((tm, MLA_HEADS * V_PAD), lambda i: (i, 0))],
        compiler_params=_cparams("parallel"),
        name="mla_keys_values",
    )(c_all.reshape(n, KV_LORA), kr_all.reshape(n, QK_ROPE), w_uk, w_uv)
    return k.reshape(b, tk, -1), v.reshape(b, tk, -1)


def _attn_kernel(q_ref, k_ref, v_ref, o_ref, *, tk, pos0, n_keys):
    tq = q_ref.shape[0]
    q = q_ref[...]
    q_pos0 = pos0 + pl.program_id(2) * tq
    q_chunk = (q_pos0 + lax.broadcasted_iota(jnp.int32, (tq, 1), 0)) // CHUNK
    k_end = jnp.minimum(((q_pos0 + tq - 1) // CHUNK + 1) * CHUNK, n_keys)
    n_blocks = (k_end + tk - 1) // tk
    n_full = jnp.minimum((q_pos0 // CHUNK + 1) * CHUNK, n_keys) // tk
    c_exp = SM_SCALE * math.log2(math.e)

    n_split = 4 if tk % (4 * 2 * LANES) == 0 else 1
    th = tk // n_split

    def block(kb, carry, masked):
        m, acc = carry
        starts = [pl.multiple_of(kb * tk + i * th, th) for i in range(n_split)]
        scores = [_dot_nt(q, k_ref[pl.ds(st, th), :]) for st in starts]
        for st, s in zip(starts, scores):
            if masked:
                k_pos = st + lax.broadcasted_iota(jnp.int32, (1, th), 1)
                visible = (k_pos // CHUNK <= q_chunk) & (k_pos < n_keys)
                s = jnp.where(visible, s, -jnp.inf)
            m_new = jnp.maximum(m, jnp.max(s, axis=-1, keepdims=True))
            p = jnp.exp2((s - m_new) * c_exp)
            alpha = jnp.exp2((m - m_new) * c_exp)
            acc = alpha * acc + _dot(p.astype(BF16), v_ref[pl.ds(st, th), :])
            m = m_new
        return m, acc

    init = (jnp.full((tq, 1), -jnp.inf, F32), jnp.zeros((tq, V_PAD), F32))
    carry = lax.fori_loop(0, n_full, functools.partial(block, masked=False), init)
    _, acc = lax.fori_loop(n_full, n_blocks, functools.partial(block, masked=True), carry)
    o_ref[...] = (acc[:, :V_HEAD] * (1.0 / acc[:, V_HEAD:V_HEAD + 1])).astype(o_ref.dtype)


def _attention(q, k, v, *, pos0, n_keys):
    b, t, _ = q.shape
    tkeys = k.shape[1]
    tq = _row_block(t, 512)
    tk = tkeys if tkeys <= ATTN_KEY_BLOCK * 2 + LANES else _row_block(tkeys, ATTN_KEY_BLOCK)
    return pl.pallas_call(
        functools.partial(_attn_kernel, tk=tk, pos0=pos0, n_keys=n_keys),
        out_shape=jax.ShapeDtypeStruct((b, t, MLA_HEADS * V_HEAD), BF16),
        grid=(b, MLA_HEADS, t // tq),
        in_specs=[pl.BlockSpec((None, tq, QK_PAD), lambda bi, h, i: (bi, i, h)),
                  pl.BlockSpec((None, tkeys, QK_PAD), lambda bi, h, i: (bi, 0, h)),
                  pl.BlockSpec((None, tkeys, V_PAD), lambda bi, h, i: (bi, 0, h))],
        out_specs=pl.BlockSpec((None, tq, V_HEAD), lambda bi, h, i: (bi, i, h)),
        compiler_params=_cparams("parallel", "parallel", "arbitrary"),
        name="mla_attention",
    )(q, k, v)


def _rope_tables(pos):
    half = QK_ROPE // 2
    inv = ROPE_THETA ** (-jnp.arange(half, dtype=F32) / half)
    ang = pos.astype(F32)[:, None] * inv[None, :]
    cos, sin = jnp.cos(ang), jnp.sin(ang)
    t = pos.shape[0]
    cs_k = jnp.concatenate([cos, cos, -sin, sin], axis=1)
    one = jnp.ones((t, QK_NOPE), F32)
    zero_n = jnp.zeros((t, QK_NOPE), F32)
    zero_p = jnp.zeros((t, QK_PAD - QK_NOPE - QK_ROPE), F32)
    cos_q = jnp.concatenate([one, cos, cos, zero_p], axis=1)
    sin_q = jnp.concatenate([zero_n, -sin, sin, zero_p], axis=1)
    return cs_k, cos_q, sin_q


def _swap_halves(w):
    half = QK_ROPE // 2
    return jnp.concatenate([w[..., half:], w[..., :half]], axis=-1)


def _prep_weights(p):
    o1 = CONV_DIM + GDN_HEADS * GDN_DV
    w = {}
    w_in = p['gdn_w_in']
    w['gdn_qkv'] = w_in[:, :, :CONV_DIM].astype(BF16)
    w['gdn_gate'] = w_in[:, :, CONV_DIM:o1].astype(BF16)
    w['gdn_ab'] = jnp.pad(w_in[:, :, o1:], ((0, 0), (0, 0), (0, LANES - 2 * GDN_HEADS))).astype(BF16)
    w['gdn_o'] = p['gdn_w_o'].astype(BF16)
    w_dkv = p['mla_w_dkv']
    w['dkv_c'] = w_dkv[:, :KV_LORA].astype(BF16)
    rope_cols = w_dkv[:, KV_LORA:]
    w['dkv_r2'] = jnp.concatenate([rope_cols, _swap_halves(rope_cols)], axis=1).astype(BF16)
    w['uk'] = p['mla_w_uk'].astype(BF16)
    w['uv'] = p['mla_w_uv'].astype(BF16)
    w['dq'] = p['mla_w_dq'].astype(BF16)
    n_b = p['mla_w_uq'].shape[0]
    uq = p['mla_w_uq'].reshape(n_b, Q_LORA, MLA_HEADS, QK_NOPE + QK_ROPE)
    pad = jnp.zeros((n_b, Q_LORA, MLA_HEADS, QK_PAD - QK_NOPE - QK_ROPE), F32)
    w['uq'] = jnp.concatenate([uq, pad], axis=-1).reshape(n_b, Q_LORA, MLA_HEADS * QK_PAD).astype(BF16)
    uq_s = jnp.concatenate([jnp.zeros_like(uq[..., :QK_NOPE]), _swap_halves(uq[..., QK_NOPE:]), pad], axis=-1)
    w['uq_swap'] = uq_s.reshape(n_b, Q_LORA, MLA_HEADS * QK_PAD).astype(BF16)
    w['mla_o'] = p['mla_w_o'].astype(BF16)
    w['peer_q'] = p['peer_w_q'].astype(BF16)
    w['peer_keys'] = p['peer_sub_keys'].astype(BF16)
    w['peer_u'] = p['peer_u'].astype(BF16)
    pv = p['peer_v'].astype(BF16)
    pv = pv.reshape(pv.shape[0], N_EXPERTS // PEER_EXPERT_BLOCK, PEER_EXPERT_BLOCK, pv.shape[-1])
    w['peer_vt'] = jnp.swapaxes(pv, 2, 3)
    return w


def _trunk(x, mods, pos0, conv_bufs, delta_states, past_c, past_kr, p, w):
    b, t, d = x.shape
    depth = mods.shape[0]
    n_a = p['gdn_w_in'].shape[0]
    new_states, new_bufs = [], []
    c_kv = k_r = k_cat = v_all = None
    n_keys = past_c.shape[1] + t
    for l in range(depth):
        m = mods[l]
        if l < n_a:
            qkv, gate, ab = _norm_matmul(x, m, p['norm1_g'][l], [w['gdn_qkv'][l], w['gdn_gate'][l], w['gdn_ab'][l]],
                                         [F32, F32, F32], mod_off=0)
            o, s = _gdn(qkv, gate, ab, conv_bufs[l], delta_states[l], p['gdn_conv_w'][l], p['gdn_a_log'][l],
                        p['gdn_dt_bias'][l], p['gdn_onorm_g'][l])
            new_bufs.append(qkv[:, t - (CONV_W - 1):, :])
            new_states.append(s)
            w_o = w['gdn_o'][l]
        else:
            j = l - n_a
            cs_k, cos_q, sin_q = _rope_tables(pos0 + jnp.arange(t))
            if j == 0:
                c_kv, k_r = _shared_latent(x, p['kv_src_g'], w['dkv_c'], w['dkv_r2'], cs_k, p['mla_kv_norm_g'])
                c_all = jnp.concatenate([past_c, c_kv], axis=1)
                kr_all = jnp.concatenate([past_kr, k_r], axis=1)
                pad = -n_keys % LANES
                c_all = jnp.pad(c_all, ((0, 0), (0, pad), (0, 0)))
                kr_all = jnp.pad(kr_all, ((0, 0), (0, pad), (0, 0)))
                k_cat, v_all = _mla_keys_values(c_all, kr_all, w['uk'], w['uv'])
            q = _mla_queries(x, m, p['norm1_g'][l], w['dq'][j], p['mla_q_norm_g'][j], w['uq'][j], w['uq_swap'][j],
                             cos_q, sin_q)
            o = _attention(q, k_cat, v_all, pos0=pos0, n_keys=n_keys)
            w_o = w['mla_o'][j]
        x, h2 = _out_residual(o, w_o, x, m, p['norm2_g'][l])
        x = _peer(h2, x, m, w['peer_q'][l], w['peer_keys'][l], w['peer_u'][l], w['peer_vt'][l], p['final_g'],
                  final_norm=(l == depth - 1))
    return x, jnp.stack(new_states), jnp.stack(new_bufs), c_kv, k_r


def kernel(x_prompt, x_sample, c_prompt, c_sample, state_delta, state_conv, cache_kv_latent, cache_k_rope, w_mod, b_mod, norm1_g, norm2_g, final_g, gdn_w_in, gdn_conv_w, gdn_a_log, gdn_dt_bias, gdn_onorm_g, gdn_w_o, kv_src_g, mla_w_dkv, mla_kv_norm_g, mla_w_uk, mla_w_uv, mla_w_dq, mla_q_norm_g, mla_w_uq, mla_w_o, peer_w_q, peer_sub_keys, peer_u, peer_v):
    p = dict(norm1_g=norm1_g, norm2_g=norm2_g, final_g=final_g, gdn_w_in=gdn_w_in, gdn_conv_w=gdn_conv_w,
             gdn_a_log=gdn_a_log, gdn_dt_bias=gdn_dt_bias, gdn_onorm_g=gdn_onorm_g, gdn_w_o=gdn_w_o,
             kv_src_g=kv_src_g, mla_w_dkv=mla_w_dkv, mla_kv_norm_g=mla_kv_norm_g, mla_w_uk=mla_w_uk,
             mla_w_uv=mla_w_uv, mla_w_dq=mla_w_dq, mla_q_norm_g=mla_q_norm_g, mla_w_uq=mla_w_uq, mla_w_o=mla_w_o,
             peer_w_q=peer_w_q, peer_sub_keys=peer_sub_keys, peer_u=peer_u, peer_v=peer_v)
    w = _prep_weights(p)
    bp, tp, d = x_prompt.shape
    bs, ts, _ = x_sample.shape
    n_a = gdn_w_in.shape[0]

    rows = -(-(bp + bs) // SUBLANES) * SUBLANES
    c_all = jnp.pad(jnp.concatenate([c_prompt, c_sample], axis=0), ((0, rows - bp - bs), (0, 0)))
    mods = _modulation(c_all, w_mod, b_mod)
    mods_p = mods[:, :bp, None, :]
    mods_s = mods[:, bp:bp + bs, None, :]

    zero_bufs = jnp.zeros((n_a, bp, CONV_W - 1, CONV_DIM), F32)
    zero_states = jnp.zeros((n_a, bp, GDN_HEADS, GDN_DK, GDN_DV), F32)
    no_c = jnp.zeros((bp, 0, KV_LORA), F32)
    no_kr = jnp.zeros((bp, 0, QK_ROPE), F32)
    y_p, p_delta, p_conv, p_lat, p_kr = _trunk(x_prompt, mods_p, 0, zero_bufs, zero_states, no_c, no_kr, p, w)
    past = cache_kv_latent.shape[1]
    y_s, s_delta, s_conv, s_lat, s_kr = _trunk(x_sample, mods_s, past, state_conv, state_delta, cache_kv_latent,
                                               cache_k_rope, p, w)
    return (y_p, y_s, p_delta, p_conv, p_lat, p_kr, s_delta, s_conv, s_lat, s_kr)
```
